```python
import math
import jax
import jax.numpy as jnp
from jax import lax
import numpy as np

D_MODEL = 2048
BATCH = 16
SEQ = 256
DEPTH = 4
DEC_BATCH = 2
DEC_SEQ = 2048
PAST_LEN = 512

GRID_W = 64
N_EVEN = (DEPTH + 1) // 2
N_ODD = DEPTH // 2
N_MOD = 6
EPS = 1e-6
NEG_INF = -1e30

SG_DIM = D_MODEL // 2
SG_GROUP_DIM = 128
SG_GROUPS = SG_DIM // SG_GROUP_DIM
SG_CHUNK = 128
NA_DIM = D_MODEL // 2
NA_HEAD_DIM = 128
NA_HEADS = NA_DIM // NA_HEAD_DIM
WIN_R = 8
WIN_C = 16
SSD_DIM = D_MODEL
SSD_HEAD_DIM = 64
SSD_HEADS = SSD_DIM // SSD_HEAD_DIM
SSD_GROUPS = 4
SSD_STATE = 128
SSD_CONV = 3
SSD_CONV_CH = SSD_DIM + 2 * SSD_GROUPS * SSD_STATE
SSD_CHUNK = 128
GLA_HEADS = 4
GLA_QK_DIM = D_MODEL // 2
GLA_V_DIM = D_MODEL
GLA_HEAD_K = GLA_QK_DIM // GLA_HEADS
GLA_HEAD_V = GLA_V_DIM // GLA_HEADS
GLA_RANK = 16
GLA_GATE_NORM = 16.0
GLA_CHUNK = 64
N_EXPERTS = 16
N_EXPERT_GROUPS = 4
EXPERTS_PER_GROUP = N_EXPERTS // N_EXPERT_GROUPS
TOP_K = 2
D_FF_EXPERT = 1024

EVEN_SPLIT = (2 * SG_DIM, NA_DIM, NA_DIM, NA_DIM)
EVEN_IN = 2 * SG_DIM + 3 * NA_DIM
EVEN_OUT = SG_DIM + NA_DIM
ODD_SPLIT = (SSD_DIM, SSD_CONV_CH, 2 * SSD_HEADS, GLA_QK_DIM, GLA_QK_DIM, GLA_V_DIM, GLA_V_DIM, 2 * GLA_RANK)
ODD_IN = SSD_DIM + SSD_CONV_CH + 2 * SSD_HEADS + 2 * GLA_QK_DIM + 2 * GLA_V_DIM + 2 * GLA_RANK
ODD_OUT = SSD_DIM + GLA_V_DIM

kernel_name = 'hybrid_prefix_diffusion_step'


def rms_norm(x, g):
    xf = x.astype(jnp.float32)
    y = xf * lax.rsqrt(jnp.mean(xf * xf, axis=-1, keepdims=True) + EPS)
    return (y * g.astype(jnp.float32)).astype(x.dtype)


def _split(x, sizes):
    cuts, acc = [], 0
    for s in sizes[:-1]:
        acc += s
        cuts.append(acc)
    return jnp.split(x, cuts, axis=-1)


def _rev(t):
    return jnp.flip(t, axis=1)


def modulation(cvec, w, b):
    m = jax.nn.silu(cvec) @ w + b
    return jnp.split(m[:, None, :], N_MOD, axis=-1)


def spatial_gating(uv, norm_g, w_s, b_s):
    b, L, _ = uv.shape
    u, v = jnp.split(uv, 2, axis=-1)
    v = rms_norm(v, norm_g).reshape(b, L // SG_CHUNK, SG_CHUNK, SG_GROUPS, SG_GROUP_DIM)
    s = jnp.einsum('gpq,bcqgd->bcpgd', w_s, v) + b_s.T[None, None, :, :, None]
    return u * s.reshape(b, L, SG_DIM)


def context_attention(q, k, v):
    s = jnp.einsum('bhqd,bhkd->bhqk', q, k, preferred_element_type=jnp.float32) * (NA_HEAD_DIM ** -0.5)
    p = jax.nn.softmax(s, axis=-1).astype(v.dtype)
    return jnp.einsum('bhqk,bhkd->bhqd', p, v)


def neighbourhood_attention(q, k, v, k_ctx, v_ctx, rpb):
    b, H, N, dh = q.shape
    rows = N // GRID_W
    wr = min(WIN_R, rows)
    r = jnp.arange(rows)
    r0 = jnp.clip(r - wr // 2, 0, rows - wr)
    key_rows = r0[:, None] + jnp.arange(wr)
    col = jnp.arange(GRID_W)
    c0 = jnp.clip(col - WIN_C // 2, 0, GRID_W - WIN_C)
    col_ok = (col[None, :] >= c0[:, None]) & (col[None, :] < c0[:, None] + WIN_C)
    dr = key_rows - r[:, None] + (WIN_R - 1)
    dc = jnp.clip(col[None, :] - col[:, None], -(WIN_C - 1), WIN_C - 1) + (WIN_C - 1)
    bias = rpb.astype(jnp.float32)[:, dr][..., dc]
    bias = bias.transpose(0, 1, 3, 2, 4)
    scale = dh ** -0.5
    qg = q.reshape(b, H, rows, GRID_W, dh)
    kg = k.reshape(b, H, rows, GRID_W, dh)[:, :, key_rows]
    vg = v.reshape(b, H, rows, GRID_W, dh)[:, :, key_rows]
    s_loc = jnp.einsum('bhrqd,bhrwkd->bhrqwk', qg, kg, preferred_element_type=jnp.float32) * scale + bias[None]
    s_loc = jnp.where(col_ok[:, None, :], s_loc, NEG_INF)
    s_ctx = jnp.einsum('bhrqd,bhcd->bhrqc', qg, k_ctx, preferred_element_type=jnp.float32) * scale
    n_loc = wr * GRID_W
    s_all = jnp.concatenate([s_loc.reshape(b, H, rows, GRID_W, n_loc), s_ctx], axis=-1)
    p = jax.nn.softmax(s_all, axis=-1).astype(v.dtype)
    p_loc = p[..., :n_loc].reshape(b, H, rows, GRID_W, wr, GRID_W)
    p_ctx = p[..., n_loc:]
    o = jnp.einsum('bhrqwk,bhrwkd->bhrqd', p_loc, vg) + jnp.einsum('bhrqc,bhcd->bhrqd', p_ctx, v_ctx)
    return o.reshape(b, H, N, dh)


def centred_depthwise_conv(x, w, bias):
    kw, ch = w.shape
    y = lax.conv_general_dilated(x, w[:, None, :].astype(x.dtype), window_strides=(1,),
                                 padding=[(kw // 2, kw // 2)],
                                 dimension_numbers=('NWC', 'WIO', 'NWC'),
                                 feature_group_count=ch)
    return y + bias


def ssd_scan(x, dt, a, bm, cm, s0):
    b, L, H, P = x.shape
    G, N = bm.shape[2], bm.shape[3]
    R = H // G
    Q = SSD_CHUNK
    nc = L // Q
    f32 = jnp.float32
    xc = x.astype(f32).reshape(b, nc, Q, G, R, P)
    dtc = dt.astype(f32).reshape(b, nc, Q, G, R)
    bc = bm.astype(f32).reshape(b, nc, Q, G, N)
    cc = cm.astype(f32).reshape(b, nc, Q, G, N)
    cum = jnp.cumsum(dtc * a.astype(f32).reshape(G, R), axis=2)
    cum_t = jnp.moveaxis(cum, 2, -1)
    seg = cum_t[..., :, None] - cum_t[..., None, :]
    causal = jnp.tril(jnp.ones((Q, Q), dtype=bool))
    decay = jnp.exp(jnp.where(causal, seg, -jnp.inf))
    cb = jnp.einsum('bcigs,bcjgs->bcgij', cc, bc)
    w = cb[:, :, :, None] * decay * jnp.moveaxis(dtc, 2, -1)[..., None, :]
    y_diag = jnp.einsum('bcgrij,bcjgrp->bcigrp', w, xc)
    to_end = jnp.exp(cum[:, :, -1:] - cum) * dtc
    states = jnp.einsum('bcjgs,bcjgrp->bcgrps', bc, xc * to_end[..., None])
    chunk_decay = jnp.exp(cum[:, :, -1])

    def step(s, inp):
        st, dec = inp
        return s * dec[..., None, None] + st, s

    final, prev = lax.scan(step, s0.astype(f32).reshape(b, G, R, P, N),
                           (jnp.moveaxis(states, 1, 0), jnp.moveaxis(chunk_decay, 1, 0)))
    prev = jnp.moveaxis(prev, 0, 1)
    y_off = jnp.einsum('bcigs,bcgrps->bcigrp', cc, prev) * jnp.exp(cum)[..., None]
    y = (y_diag + y_off).reshape(b, L, H, P)
    return y.astype(x.dtype), final.reshape(b, H, P, N).astype(x.dtype)


def gla_scan(q, k, v, g, s0):
    b, L, H, K = q.shape
    V = v.shape[-1]
    Q = GLA_CHUNK
    nc = L // Q
    f32 = jnp.float32
    qc = q.astype(f32).reshape(b, nc, Q, H, K)
    kc = k.astype(f32).reshape(b, nc, Q, H, K)
    vc = v.astype(f32).reshape(b, nc, Q, H, V)
    cum = jnp.cumsum(g.astype(f32).reshape(b, nc, Q, H, K), axis=2)
    q_e = qc * jnp.exp(cum)
    k_e = kc * jnp.exp(-cum)
    causal = jnp.tril(jnp.ones((Q, Q), dtype=bool))
    att = jnp.where(causal, jnp.einsum('bcihk,bcjhk->bchij', q_e, k_e), 0.0)
    o_intra = jnp.einsum('bchij,bcjhv->bcihv', att, vc)
    last = cum[:, :, -1]
    k_end = kc * jnp.exp(last[:, :, None] - cum)
    states = jnp.einsum('bcjhk,bcjhv->bchkv', k_end, vc)

    def step(s, inp):
        st, dec = inp
        return s * dec[..., None] + st, s

    final, prev = lax.scan(step, s0.astype(f32),
                           (jnp.moveaxis(states, 1, 0), jnp.moveaxis(jnp.exp(last), 1, 0)))
    prev = jnp.moveaxis(prev, 0, 1)
    o_inter = jnp.einsum('bcihk,bchkv->bcihv', q_e, prev)
    o = (o_intra + o_inter).reshape(b, L, H, V)
    return o.astype(q.dtype), final.astype(q.dtype)


def even_mixer(hn, prm, j, kv_ctx):
    b, L, _ = hn.shape
    uv, q, k, v = _split(hn @ prm['w_in_even'][j], EVEN_SPLIT)
    a_out = spatial_gating(jax.nn.gelu(uv), prm['sg_norm'][j], prm['sg_w'][j], prm['sg_b'][j])

    def heads(t):
        return t.reshape(b, L, NA_HEADS, NA_HEAD_DIM).transpose(0, 2, 1, 3)

    q, k, v = heads(q), heads(k), heads(v)
    if kv_ctx is None:
        o = context_attention(q, k, v)
        kv = jnp.stack([k, v], axis=1)
    else:
        o = neighbourhood_attention(q, k, v, kv_ctx[:, 0], kv_ctx[:, 1], prm['na_rpb'][j])
        kv = None
    o = o.transpose(0, 2, 1, 3).reshape(b, L, NA_DIM)
    out = jnp.concatenate([a_out, o], axis=-1) @ prm['w_out_even'][j]
    return out, kv


def odd_mixer(hn, prm, j, s_ssm, s_gla):
    b, L, _ = hn.shape
    f32 = jnp.float32
    z, xbc, dt_raw, q, k, v, gate, a_lr = _split(hn @ prm['w_in_odd'][j], ODD_SPLIT)
    xbc = jax.nn.silu(centred_depthwise_conv(xbc, prm['ssd_conv_w'][j], prm['ssd_conv_b'][j]))
    xs, bm, cm = _split(xbc, (SSD_DIM, SSD_GROUPS * SSD_STATE, SSD_GROUPS * SSD_STATE))
    xs = xs.reshape(b, L, SSD_HEADS, SSD_HEAD_DIM)
    bm = bm.reshape(b, L, SSD_GROUPS, SSD_STATE)
    cm = cm.reshape(b, L, SSD_GROUPS, SSD_STATE)
    dt = jax.nn.softplus(dt_raw.reshape(b, L, 2, SSD_HEADS).astype(f32) + prm['ssd_dt_bias'][j].astype(f32))
    a = -jnp.exp(prm['ssd_a_log'][j].astype(f32))
    y_f, sf = ssd_scan(xs, dt[:, :, 0], a[0], bm, cm, s_ssm[:, 0])
    y_b, sb = ssd_scan(_rev(xs), _rev(dt[:, :, 1]), a[1], _rev(bm), _rev(cm), s_ssm[:, 1])
    y = y_f + _rev(y_b) + prm['ssd_d'][j][:, None] * xs
    y_c = rms_norm(y.reshape(b, L, SSD_DIM) * jax.nn.silu(z), prm['ssd_norm'][j])
    q = q.reshape(b, L, GLA_HEADS, GLA_HEAD_K) * (GLA_HEAD_K ** -0.5)
    k = k.reshape(b, L, GLA_HEADS, GLA_HEAD_K)
    v = v.reshape(b, L, GLA_HEADS, GLA_HEAD_V)
    a_lr = a_lr.reshape(b, L, 2, GLA_RANK)
    g_lin = jnp.einsum('bldr,drk->bldk', a_lr, prm['gla_w_a2'][j]) + prm['gla_b_a'][j]
    g = (jax.nn.log_sigmoid(g_lin.astype(f32)) / GLA_GATE_NORM).reshape(b, L, 2, GLA_HEADS, GLA_HEAD_K)
    o_f, gf = gla_scan(q, k, v, g[:, :, 0], s_gla[:, 0])
    o_b, gb = gla_scan(_rev(q), _rev(k), _rev(v), _rev(g[:, :, 1]), s_gla[:, 1])
    o = rms_norm(o_f + _rev(o_b), prm['gla_norm'][j]).reshape(b, L, GLA_V_DIM)
    y_d = o * jax.nn.silu(gate)
    out = jnp.concatenate([y_c, y_d], axis=-1) @ prm['w_out_odd'][j]
    return out, jnp.stack([sf, sb], axis=1), jnp.stack([gf, gb], axis=1)


def moe(h, w_router, b_router, w_gate, w_up, w_down):
    b, L, D = h.shape
    t = h.reshape(b * L, D)
    scores = jax.nn.sigmoid((t @ w_router).astype(jnp.float32))
    sel = scores + b_router.astype(jnp.float32)
    grp_score = jnp.sum(lax.top_k(sel.reshape(-1, N_EXPERT_GROUPS, EXPERTS_PER_GROUP), TOP_K)[0], axis=-1)
    g_idx = jnp.argmax(grp_score, axis=-1)
    gmask = jnp.repeat(g_idx[:, None] == jnp.arange(N_EXPERT_GROUPS), EXPERTS_PER_GROUP, axis=1)
    _, top_idx = lax.top_k(jnp.where(gmask, sel, -jnp.inf), TOP_K)
    top_w = jnp.take_along_axis(scores, top_idx, axis=-1)
    top_w = top_w / jnp.sum(top_w, axis=-1, keepdims=True)
    gates = jnp.sum(jax.nn.one_hot(top_idx, N_EXPERTS, dtype=jnp.float32) * top_w[..., None], axis=1)
    hg = jnp.einsum('td,edf->tef', t, w_gate)
    hu = jnp.einsum('td,edf->tef', t, w_up)
    act = jax.nn.silu(hg) * hu * gates[:, :, None].astype(t.dtype)
    return jnp.einsum('tef,efd->td', act, w_down).reshape(b, L, D)


def trunk(x, cvec, prm, cache_kv, state_ssm, state_gla):
    is_ctx = cache_kv is None
    b = x.shape[0]
    kv_list, ssm_list, gla_list = [], [], []
    for i in range(DEPTH):
        sh1, sc1, g1, sh2, sc2, g2 = modulation(cvec, prm['w_mod'][i], prm['b_mod'][i])
        hn = rms_norm(x, prm['norm_mix'][i]) * (1 + sc1) + sh1
        j = i // 2
        if i % 2 == 0:
            out, kv = even_mixer(hn, prm, j, None if is_ctx else cache_kv[:, j])
            if is_ctx:
                kv_list.append(kv)
        else:
            if is_ctx:
                s_ssm = jnp.zeros((b, 2, SSD_HEADS, SSD_HEAD_DIM, SSD_STATE), x.dtype)
                s_gla = jnp.zeros((b, 2, GLA_HEADS, GLA_HEAD_K, GLA_HEAD_V), x.dtype)
            else:
                s_ssm = state_ssm[:, j]
                s_gla = state_gla[:, j]
            out, s_ssm, s_gla = odd_mixer(hn, prm, j, s_ssm, s_gla)
            if is_ctx:
                ssm_list.append(s_ssm)
                gla_list.append(s_gla)
        x = x + g1 * out
        hn = rms_norm(x, prm['norm_ffn'][i]) * (1 + sc2) + sh2
        x = x + g2 * moe(hn, prm['w_router'], prm['b_router'], prm['w_gate'][i], prm['w_up'][i], prm['w_down'][i])
    return rms_norm(x, prm['norm_final']), kv_list, ssm_list, gla_list


def setup_inputs(seed: int = 0) -> dict:
    key = jax.random.key(seed)
    ks = iter(jax.random.split(key, 48))
    f32 = jnp.float32

    def nrm(shape, s):
        return jax.random.normal(next(ks), shape, f32) * s

    def gain(shape):
        return 1.0 + nrm(shape, 0.01)

    D = D_MODEL
    u = jax.random.uniform(next(ks), (N_ODD, 2, SSD_HEADS), f32)
    dt0 = jnp.exp(u * (math.log(0.1) - math.log(0.001)) + math.log(0.001))
    return {
        'x_prompt': nrm((BATCH, SEQ, D), 1.0),
        'x_sample': nrm((DEC_BATCH, DEC_SEQ, D), 1.0),
        'cache_kv': nrm((DEC_BATCH, N_EVEN, 2, NA_HEADS, PAST_LEN, NA_HEAD_DIM), 1.0),
        'state_ssm': nrm((DEC_BATCH, N_ODD, 2, SSD_HEADS, SSD_HEAD_DIM, SSD_STATE), 0.1),
        'state_gla': nrm((DEC_BATCH, N_ODD, 2, GLA_HEADS, GLA_HEAD_K, GLA_HEAD_V), 0.1),
        'c': nrm((DEC_BATCH, D), 1.0),
        'c_ctx': nrm((D,), 1.0),
        'w_mod': nrm((DEPTH, D, N_MOD * D), 0.5 * D ** -0.5),
        'b_mod': nrm((DEPTH, N_MOD * D), 0.01),
        'norm_mix': gain((DEPTH, D)),
        'norm_ffn': gain((DEPTH, D)),
        'norm_final': gain((D,)),
        'w_in_even': nrm((N_EVEN, D, EVEN_IN), D ** -0.5),
        'w_out_even': nrm((N_EVEN, EVEN_OUT, D), EVEN_OUT ** -0.5),
        'sg_norm': gain((N_EVEN, SG_DIM)),
        'sg_w': nrm((N_EVEN, SG_GROUPS, SG_CHUNK, SG_CHUNK), SG_CHUNK ** -0.5),
        'sg_b': nrm((N_EVEN, SG_GROUPS, SG_CHUNK), 0.02),
        'na_rpb': nrm((N_EVEN, NA_HEADS, 2 * WIN_R - 1, 2 * WIN_C - 1), 0.02),
        'w_in_odd': nrm((N_ODD, D, ODD_IN), D ** -0.5),
        'w_out_odd': nrm((N_ODD, ODD_OUT, D), ODD_OUT ** -0.5),
        'ssd_conv_w': nrm((N_ODD, SSD_CONV, SSD_CONV_CH), SSD_CONV ** -0.5),
        'ssd_conv_b': nrm((N_ODD, SSD_CONV_CH), 0.02),
        'ssd_dt_bias': dt0 + jnp.log(-jnp.expm1(-dt0)),
        'ssd_a_log': jnp.log(jax.random.uniform(next(ks), (N_ODD, 2, SSD_HEADS), f32, 1.0, 16.0)),
        'ssd_d': gain((N_ODD, SSD_HEADS)),
        'ssd_norm': gain((N_ODD, SSD_DIM)),
        'gla_w_a2': nrm((N_ODD, 2, GLA_RANK, GLA_QK_DIM), GLA_RANK ** -0.5),
        'gla_b_a': nrm((N_ODD, 2, GLA_QK_DIM), 0.02),
        'gla_norm': gain((N_ODD, GLA_HEAD_V)),
        'w_router': nrm((D, N_EXPERTS), D ** -0.5),
        'b_router': nrm((N_EXPERTS,), 0.01),
        'w_gate': nrm((DEPTH, N_EXPERTS, D, D_FF_EXPERT), D ** -0.5),
        'w_up': nrm((DEPTH, N_EXPERTS, D, D_FF_EXPERT), D ** -0.5),
        'w_down': nrm((DEPTH, N_EXPERTS, D_FF_EXPERT, D), D_FF_EXPERT ** -0.5),
    }


def reference(x_prompt, x_sample, cache_kv, state_ssm, state_gla, c, c_ctx,
              w_mod, b_mod, norm_mix, norm_ffn, norm_final,
              w_in_even, w_out_even, sg_norm, sg_w, sg_b, na_rpb,
              w_in_odd, w_out_odd, ssd_conv_w, ssd_conv_b, ssd_dt_bias, ssd_a_log, ssd_d, ssd_norm,
              gla_w_a2, gla_b_a, gla_norm, w_router, b_router, w_gate, w_up, w_down):
    prm = dict(w_mod=w_mod, b_mod=b_mod, norm_mix=norm_mix, norm_ffn=norm_ffn, norm_final=norm_final,
               w_in_even=w_in_even, w_out_even=w_out_even, sg_norm=sg_norm, sg_w=sg_w, sg_b=sg_b,
               na_rpb=na_rpb, w_in_odd=w_in_odd, w_out_odd=w_out_odd, ssd_conv_w=ssd_conv_w,
               ssd_conv_b=ssd_conv_b, ssd_dt_bias=ssd_dt_bias, ssd_a_log=ssd_a_log, ssd_d=ssd_d,
               ssd_norm=ssd_norm, gla_w_a2=gla_w_a2, gla_b_a=gla_b_a, gla_norm=gla_norm,
               w_router=w_router, b_router=b_router, w_gate=w_gate, w_up=w_up, w_down=w_down)
    y_prompt, kv_list, ssm_list, gla_list = trunk(x_prompt, c_ctx[None, :], prm, None, None, None)
    new_cache_kv = jnp.stack(kv_list, axis=1)
    new_state_ssm = jnp.stack(ssm_list, axis=1)
    new_state_gla = jnp.stack(gla_list, axis=1)
    y_sample, _, _, _ = trunk(x_sample, c, prm, cache_kv, state_ssm, state_gla)
    return (y_prompt, y_sample, new_cache_kv, new_state_ssm, new_state_gla)
```

```python
import functools

import numpy as np
import jax
import jax.numpy as jnp
from jax import lax
from jax.experimental import pallas as pl
from jax.experimental.pallas import tpu as pltpu

F32 = jnp.float32
BF16 = jnp.bfloat16

D_MODEL = 2048
DEPTH = 4
N_MOD = 6
EPS = 1e-6
NEG_INF = -1e30

N_CTX_SEQ = 16
L_CTX = 256
N_LAT_SEQ = 2
L_LAT = 2048
PAST_LEN = 512
T_CTX = N_CTX_SEQ * L_CTX
T_LAT = N_LAT_SEQ * L_LAT
T_ALL = T_CTX + T_LAT
N_SEQ = N_CTX_SEQ + N_LAT_SEQ
N_MOD_ROWS = 8

GRID_W = 64
SG_DIM = 1024
SG_GROUPS = 8
SG_CHUNK = 128
NA_DIM = 1024
NA_HEADS = 8
NA_HEAD_DIM = 128
WIN_R = 8
WIN_C = 16
NA_ROWS = L_LAT // GRID_W
EVEN_IN = 2 * SG_DIM + 3 * NA_DIM

SSD_DIM = 2048
SSD_HEADS = 32
SSD_HEAD_DIM = 64
SSD_GROUPS = 4
SSD_STATE = 128
SSD_CHUNK = 128
SSD_BC = SSD_GROUPS * SSD_STATE
SSD_CONV_CH = SSD_DIM + 2 * SSD_BC
GLA_HEADS = 4
GLA_QK_DIM = 1024
GLA_V_DIM = 2048
GLA_HEAD_K = 256
GLA_HEAD_V = 512
GLA_RANK = 16
GLA_GATE_NORM = 16.0
GLA_CHUNK = 64
SMALL_W = 128
ODD_MAIN = SSD_DIM + 2 * GLA_V_DIM + 2 * GLA_QK_DIM + SSD_CONV_CH
ODD_COL_XBC = SSD_DIM + 2 * GLA_V_DIM + 2 * GLA_QK_DIM

N_EXPERTS = 16
N_EXPERT_GROUPS = 4
EXPERTS_PER_GROUP = 4
D_FF = 1024
MOE_TM = 256
MOE_TILES = (2 * T_ALL) // MOE_TM + N_EXPERTS
MOE_ROWS = MOE_TILES * MOE_TM

TM = 512
VMEM_LIMIT = 56 * 1024 * 1024


def _cparams(sem):
    return pltpu.CompilerParams(dimension_semantics=sem, vmem_limit_bytes=VMEM_LIMIT)


def _mod_row(i, tm):
    nct = T_CTX // tm
    per = L_LAT // tm
    return jnp.where(i < nct, 0, 1 + (i - nct) // per)


def _silu(x):
    return x * jax.nn.sigmoid(x)


def _softplus(x):
    return jnp.maximum(x, 0.0) + jnp.log1p(jnp.exp(-jnp.abs(x)))


def _gelu_tanh(x):
    return 0.5 * x * (1.0 + jnp.tanh(np.sqrt(2.0 / np.pi).astype(np.float32) * (x + 0.044715 * (x * x * x))))


def _split3(x):
    hi = x.astype(BF16)
    r1 = x - hi.astype(F32)
    mid = r1.astype(BF16)
    lo = (r1 - mid.astype(F32)).astype(BF16)
    return hi, mid, lo


def _dot(a, b):
    return jnp.dot(a, b, preferred_element_type=F32)


def _dot_nt(a, b):
    return lax.dot_general(a, b, (((1,), (1,)), ((), ())), preferred_element_type=F32)


def _mask_dot(mask_bf16, x):
    hi, mid, lo = _split3(x)
    return _dot(mask_bf16, hi) + _dot(mask_bf16, mid) + _dot(mask_bf16, lo)


def _mod_kernel(c_ref, w_ref, b_ref, o_ref):
    s = _silu(c_ref[...]).astype(BF16)
    o_ref[...] = _dot(s, w_ref[...].astype(BF16)) + b_ref[...]


def _modulation(c_rows, w_mod, b_mod):
    tn = 1024
    n = N_MOD * D_MODEL
    return pl.pallas_call(
        _mod_kernel,
        grid=(DEPTH, n // tn),
        in_specs=[pl.BlockSpec((N_MOD_ROWS, D_MODEL), lambda l, j: (0, 0)),
                  pl.BlockSpec((None, D_MODEL, tn), lambda l, j: (l, 0, j)),
                  pl.BlockSpec((None, 1, tn), lambda l, j: (l, 0, j))],
        out_specs=pl.BlockSpec((None, N_MOD_ROWS, tn), lambda l, j: (l, 0, j)),
        out_shape=jax.ShapeDtypeStruct((DEPTH, N_MOD_ROWS, n), F32),
        compiler_params=_cparams(("parallel", "parallel")),
        name="modulation",
    )(c_rows, w_mod, b_mod.reshape(DEPTH, 1, n))


def _norm_mod(x, g, sc, sh):
    y = x * lax.rsqrt(jnp.mean(x * x, axis=-1, keepdims=True) + EPS) * g
    return y * (1.0 + sc) + sh


def _inproj_kernel(x_ref, g_ref, sc_ref, sh_ref, w_ref, o_ref, hn_ref):
    @pl.when(pl.program_id(1) == 0)
    def _():
        hn_ref[...] = _norm_mod(x_ref[...], g_ref[...], sc_ref[...], sh_ref[...]).astype(BF16)

    o_ref[...] = _dot(hn_ref[...], w_ref[...])


def _inproj(x, g, sc, sh, w, name):
    n = w.shape[1]
    tn = min(n, 512)
    mod_spec = pl.BlockSpec((None, 1, D_MODEL), lambda i, j: (_mod_row(i, TM), 0, 0))
    return pl.pallas_call(
        _inproj_kernel,
        grid=(T_ALL // TM, n // tn),
        in_specs=[pl.BlockSpec((TM, D_MODEL), lambda i, j: (i, 0)),
                  pl.BlockSpec((1, D_MODEL), lambda i, j: (0, 0)),
                  mod_spec, mod_spec,
                  pl.BlockSpec((D_MODEL, tn), lambda i, j: (0, j))],
        out_specs=pl.BlockSpec((TM, tn), lambda i, j: (i, j)),
        out_shape=jax.ShapeDtypeStruct((T_ALL, n), F32),
        scratch_shapes=[pltpu.VMEM((TM, D_MODEL), BF16)],
        compiler_params=_cparams(("parallel", "arbitrary")),
        name=name,
    )(x, g.reshape(1, D_MODEL), sc, sh, w)


def _outproj_kernel(a_ref, b_ref, w1_ref, w2_ref, x_ref, g_ref, o_ref):
    acc = _dot(a_ref[...], w1_ref[...]) + _dot(b_ref[...], w2_ref[...])
    o_ref[...] = x_ref[...] + g_ref[...] * acc


def _outproj(a, b, w, x, gate, name):
    k1 = a.shape[1]
    tn = 512
    return pl.pallas_call(
        _outproj_kernel,
        grid=(T_ALL // TM, D_MODEL // tn),
        in_specs=[pl.BlockSpec((TM, k1), lambda i, j: (i, 0)),
                  pl.BlockSpec((TM, k1), lambda i, j: (i, 0)),
                  pl.BlockSpec((k1, tn), lambda i, j: (0, j)),
                  pl.BlockSpec((k1, tn), lambda i, j: (1, j)),
                  pl.BlockSpec((TM, tn), lambda i, j: (i, j)),
                  pl.BlockSpec((None, 1, tn), lambda i, j: (_mod_row(i, TM), 0, j))],
        out_specs=pl.BlockSpec((TM, tn), lambda i, j: (i, j)),
        out_shape=jax.ShapeDtypeStruct((T_ALL, D_MODEL), F32),
        compiler_params=_cparams(("parallel", "parallel")),
        name=name,
    )(a, b, w, w, x, gate)


def _sgate_kernel(u_ref, v_ref, g_ref, ws_ref, bs_ref, o_ref):
    u = _gelu_tanh(u_ref[...])
    v = _gelu_tanh(v_ref[...])
    v = v * lax.rsqrt(jnp.mean(v * v, axis=-1, keepdims=True) + EPS) * g_ref[...]
    vb = v.astype(BF16)
    for c in range(TM // SG_CHUNK):
        rows = slice(c * SG_CHUNK, (c + 1) * SG_CHUNK)
        for g in range(SG_GROUPS):
            cols = slice(g * 128, (g + 1) * 128)
            s = _dot(ws_ref[g], vb[rows, cols]) + bs_ref[g]
            o_ref[rows, cols] = (u[rows, cols] * s).astype(o_ref.dtype)


def _spatial_gating(h, sg_norm, sg_w, sg_b):
    bias = jnp.broadcast_to(sg_b[:, :, None], (SG_GROUPS, SG_CHUNK, 128))
    return pl.pallas_call(
        _sgate_kernel,
        grid=(T_ALL // TM,),
        in_specs=[pl.BlockSpec((TM, SG_DIM), lambda i: (i, 0)),
                  pl.BlockSpec((TM, SG_DIM), lambda i: (i, 1)),
                  pl.BlockSpec((1, SG_DIM), lambda i: (0, 0)),
                  pl.BlockSpec((SG_GROUPS, SG_CHUNK, SG_CHUNK), lambda i: (0, 0, 0)),
                  pl.BlockSpec((SG_GROUPS, SG_CHUNK, 128), lambda i: (0, 0, 0))],
        out_specs=pl.BlockSpec((TM, SG_DIM), lambda i: (i, 0)),
        out_shape=jax.ShapeDtypeStruct((T_ALL, SG_DIM), BF16),
        compiler_params=_cparams(("parallel",)),
        name="spatial_gating",
    )(h, h, sg_norm.reshape(1, SG_DIM), sg_w.astype(BF16), bias)


def _softmax_rows(s):
    e = jnp.exp(s - jnp.max(s, axis=-1, keepdims=True))
    return e / jnp.sum(e, axis=-1, keepdims=True)


def _ctx_attn_kernel(q_ref, k_ref, v_ref, o_ref, kv_ref):
    k = k_ref[...]
    v = v_ref[...]
    s = _dot_nt(q_ref[...].astype(BF16), k.astype(BF16)) * (NA_HEAD_DIM ** -0.5)
    p = _softmax_rows(s).astype(BF16)
    o_ref[...] = _dot(p, v.astype(BF16)).astype(o_ref.dtype)
    kv_ref[0, 0] = k
    kv_ref[1, 0] = v


def _context_attention(h):
    qc = 2 * SG_DIM // NA_HEAD_DIM
    return pl.pallas_call(
        _ctx_attn_kernel,
        grid=(N_CTX_SEQ, NA_HEADS),
        in_specs=[pl.BlockSpec((L_CTX, NA_HEAD_DIM), lambda b, hh: (b, qc + hh)),
                  pl.BlockSpec((L_CTX, NA_HEAD_DIM), lambda b, hh: (b, qc + NA_HEADS + hh)),
                  pl.BlockSpec((L_CTX, NA_HEAD_DIM), lambda b, hh: (b, qc + 2 * NA_HEADS + hh))],
        out_specs=[pl.BlockSpec((L_CTX, NA_HEAD_DIM), lambda b, hh: (b, hh)),
                   pl.BlockSpec((None, 2, 1, L_CTX, NA_HEAD_DIM), lambda b, hh: (b, 0, hh, 0, 0))],
        out_shape=[jax.ShapeDtypeStruct((T_CTX, NA_DIM), BF16),
                   jax.ShapeDtypeStruct((N_CTX_SEQ, 2, NA_HEADS, L_CTX, NA_HEAD_DIM), F32)],
        compiler_params=_cparams(("parallel", "parallel")),
        name="context_attention",
    )(h, h, h)


def _na_bias_table(rpb):
    col = np.arange(GRID_W)
    c0 = np.clip(col - WIN_C // 2, 0, GRID_W - WIN_C)
    col_ok = (col[None, :] >= c0[:, None]) & (col[None, :] < c0[:, None] + WIN_C)
    dc = np.clip(col[None, :] - col[:, None], -(WIN_C - 1), WIN_C - 1) + (WIN_C - 1)
    dr = np.arange(WIN_R)[:, None] + np.arange(WIN_R)[None, :]
    b = rpb.astype(F32)[:, dr][..., dc]
    b = jnp.where(col_ok[None, None, None], b, NEG_INF)
    return b.transpose(0, 1, 3, 2, 4).reshape(NA_HEADS, WIN_R, GRID_W, WIN_R * GRID_W)


def _na_attn_kernel(q_ref, k_ref, v_ref, kc_ref, vc_ref, bias_ref, o_ref, kb, vb, kcb, vcb):
    kb[...] = k_ref[...].astype(BF16)
    vb[...] = v_ref[...].astype(BF16)
    kcb[...] = kc_ref[...].astype(BF16)
    vcb[...] = vc_ref[...].astype(BF16)
    scale = NA_HEAD_DIM ** -0.5
    win = WIN_R * GRID_W

    def row(r, carry):
        r0 = jnp.clip(r - WIN_R // 2, 0, NA_ROWS - WIN_R)
        d0 = r0 - r + (WIN_R - 1)
        qs = pl.multiple_of(r * GRID_W, GRID_W)
        ks = pl.multiple_of(r0 * GRID_W, GRID_W)
        q = q_ref[pl.ds(qs, GRID_W), :].astype(BF16)
        s_loc = _dot_nt(q, kb[pl.ds(ks, win), :]) * scale + bias_ref[d0]
        s_ctx = _dot_nt(q, kcb[...]) * scale
        m = jnp.maximum(jnp.max(s_loc, axis=-1, keepdims=True), jnp.max(s_ctx, axis=-1, keepdims=True))
        e_loc = jnp.exp(s_loc - m)
        e_ctx = jnp.exp(s_ctx - m)
        den = jnp.sum(e_loc, axis=-1, keepdims=True) + jnp.sum(e_ctx, axis=-1, keepdims=True)
        o = _dot((e_loc / den).astype(BF16), vb[pl.ds(ks, win), :]) + _dot((e_ctx / den).astype(BF16), vcb[...])
        o_ref[pl.ds(qs, GRID_W), :] = o.astype(o_ref.dtype)
        return carry

    lax.fori_loop(0, NA_ROWS, row, 0)


def _neighbourhood_attention(h, cache_kv, j, bias):
    qc = 2 * SG_DIM // NA_HEAD_DIM
    rb = T_CTX // L_LAT
    cache_spec = lambda kv: pl.BlockSpec((None, None, None, None, PAST_LEN, NA_HEAD_DIM),
                                         lambda b, hh: (b, j, kv, hh, 0, 0))
    return pl.pallas_call(
        _na_attn_kernel,
        grid=(N_LAT_SEQ, NA_HEADS),
        in_specs=[pl.BlockSpec((L_LAT, NA_HEAD_DIM), lambda b, hh: (rb + b, qc + hh)),
                  pl.BlockSpec((L_LAT, NA_HEAD_DIM), lambda b, hh: (rb + b, qc + NA_HEADS + hh)),
                  pl.BlockSpec((L_LAT, NA_HEAD_DIM), lambda b, hh: (rb + b, qc + 2 * NA_HEADS + hh)),
                  cache_spec(0), cache_spec(1),
                  pl.BlockSpec((None, WIN_R, GRID_W, WIN_R * GRID_W), lambda b, hh: (hh, 0, 0, 0))],
        out_specs=pl.BlockSpec((L_LAT, NA_HEAD_DIM), lambda b, hh: (b, hh)),
        out_shape=jax.ShapeDtypeStruct((T_LAT, NA_DIM), BF16),
        scratch_shapes=[pltpu.VMEM((L_LAT, NA_HEAD_DIM), BF16), pltpu.VMEM((L_LAT, NA_HEAD_DIM), BF16),
                        pltpu.VMEM((PAST_LEN, NA_HEAD_DIM), BF16), pltpu.VMEM((PAST_LEN, NA_HEAD_DIM), BF16)],
        compiler_params=_cparams(("parallel", "parallel")),
        name="neighbourhood_attention",
    )(h, h, h, cache_kv, cache_kv, bias)


CONV_ROWS = 256
CONV_COLS = 512


def _conv_kernel(x_ref, p_ref, n_ref, w_ref, b_ref, o_ref):
    i = pl.program_id(0)
    per = L_LAT // CONV_ROWS
    nct = T_CTX // CONV_ROWS
    pos = jnp.where(i < nct, 0, (i - nct) % per)
    last = jnp.where(i < nct, 0, per - 1)
    x = x_ref[...]
    prev_row = jnp.where(pos == 0, 0.0, p_ref[7:8, :])
    next_row = jnp.where(pos == last, 0.0, n_ref[0:1, :])
    ridx = lax.broadcasted_iota(jnp.int32, x.shape, 0)
    x_dn = jnp.where(ridx == 0, prev_row, pltpu.roll(x, 1, 0))
    x_up = jnp.where(ridx == CONV_ROWS - 1, next_row, pltpu.roll(x, CONV_ROWS - 1, 0))
    y = x_dn * w_ref[0:1, :] + x * w_ref[1:2, :] + x_up * w_ref[2:3, :] + b_ref[...]
    o_ref[...] = _silu(y)


def _conv_silu(h, conv_w, conv_b):
    cb0 = ODD_COL_XBC // CONV_COLS
    rb = CONV_ROWS // 8
    nblk8 = T_ALL // 8
    w8 = jnp.zeros((8, SSD_CONV_CH), F32).at[:3].set(conv_w)
    return pl.pallas_call(
        _conv_kernel,
        grid=(T_ALL // CONV_ROWS, SSD_CONV_CH // CONV_COLS),
        in_specs=[pl.BlockSpec((CONV_ROWS, CONV_COLS), lambda i, j: (i, cb0 + j)),
                  pl.BlockSpec((8, CONV_COLS), lambda i, j: (jnp.maximum(i * rb - 1, 0), cb0 + j)),
                  pl.BlockSpec((8, CONV_COLS), lambda i, j: (jnp.minimum((i + 1) * rb, nblk8 - 1), cb0 + j)),
                  pl.BlockSpec((8, CONV_COLS), lambda i, j: (0, j)),
                  pl.BlockSpec((1, CONV_COLS), lambda i, j: (0, j))],
        out_specs=pl.BlockSpec((CONV_ROWS, CONV_COLS), lambda i, j: (i, j)),
        out_shape=jax.ShapeDtypeStruct((T_ALL, SSD_CONV_CH), F32),
        compiler_params=_cparams(("parallel", "parallel")),
        name="conv_silu",
    )(h, h, h, w8, conv_b.reshape(1, SSD_CONV_CH))


def _seq_info(t, chunk):
    nct = L_CTX // chunk
    ncl = L_LAT // chunk
    nctx = N_CTX_SEQ * nct
    is_ctx = t < nctx
    tl = jnp.maximum(t - nctx, 0)
    seq = jnp.where(is_ctx, t // nct, N_CTX_SEQ + tl // ncl)
    c = jnp.where(is_ctx, t % nct, tl % ncl)
    n = jnp.where(is_ctx, nct, ncl)
    first = jnp.where(is_ctx, (t // nct) * nct, nctx + (tl // ncl) * ncl)
    return seq, c, n, first


def _row_blk(t, chunk, rev):
    if not rev:
        return t
    _, c, n, first = _seq_info(t, chunk)
    return first + (n - 1 - c)


def _seq_of(t, chunk):
    return _seq_info(t, chunk)[0]


def _ssd_kernel(rev, *refs):
    if rev:
        (xa_ref, sm_ref, dtb_ref, alog_ref, s0_ref, yf_ref, z_ref, dexp_ref, gn_ref,
         y_ref, sfin_ref, s_scr, y_scr) = refs
    else:
        xa_ref, sm_ref, dtb_ref, alog_ref, s0_ref, y_ref, sfin_ref, s_scr, y_scr = refs
    t = pl.program_id(0)
    _, c, n, _ = _seq_info(t, SSD_CHUNK)
    q = SSD_CHUNK
    d = 1 if rev else 0

    @pl.when(c == 0)
    def _():
        s_scr[...] = s0_ref[...]

    ii = lax.broadcasted_iota(jnp.int32, (q, q), 0)
    jj = lax.broadcasted_iota(jnp.int32, (q, q), 1)
    mask = (jj >= ii) if rev else (jj <= ii)
    tri = mask.astype(BF16)

    dt = _softplus(sm_ref[...] + dtb_ref[...])
    dta = dt * (-jnp.exp(alog_ref[...]))
    cum = _mask_dot(tri, dta)
    tot = cum[0:1, :] if rev else cum[q - 1:q, :]
    te = jnp.exp(tot - cum) * dt
    ecum = jnp.exp(cum)
    etot = jnp.exp(tot)
    cum_t = cum.T
    dt_t = dt.T

    for g in range(SSD_GROUPS):
        bg = xa_ref[:, SSD_DIM + g * SSD_STATE:SSD_DIM + (g + 1) * SSD_STATE]
        cg = xa_ref[:, SSD_DIM + SSD_BC + g * SSD_STATE:SSD_DIM + SSD_BC + (g + 1) * SSD_STATE]
        bg_b = bg.astype(BF16)
        cg_b = cg.astype(BF16)
        cb = _dot_nt(cg_b, bg_b)
        for r in range(SSD_HEADS // SSD_GROUPS):
            hh = g * (SSD_HEADS // SSD_GROUPS) + r
            col = d * SSD_HEADS + hh
            hs = slice(hh * SSD_HEAD_DIM, (hh + 1) * SSD_HEAD_DIM)
            xh = xa_ref[:, hs]
            sh = s_scr[hs, :]
            seg = cum[:, col:col + 1] - cum_t[col:col + 1, :]
            w = cb * jnp.where(mask, jnp.exp(seg), 0.0) * dt_t[col:col + 1, :]
            y_diag = _dot(w.astype(BF16), xh.astype(BF16))
            y_off = _dot_nt(cg_b, sh.astype(BF16)) * ecum[:, col:col + 1]
            y_scr[:, hs] = y_diag + y_off
            xw = (xh * te[:, col:col + 1]).astype(BF16)
            st = lax.dot_general(xw, bg_b, (((0,), (0,)), ((), ())), preferred_element_type=F32)
            s_scr[hs, :] = sh * etot[0:1, col:col + 1] + st

    if rev:
        y = yf_ref[...] + y_scr[...] + dexp_ref[...] * xa_ref[:, 0:SSD_DIM]
        y = y * _silu(z_ref[...])
        y = y * lax.rsqrt(jnp.mean(y * y, axis=-1, keepdims=True) + EPS) * gn_ref[...]
        y_ref[...] = y.astype(y_ref.dtype)
    else:
        y_ref[...] = y_scr[...]

    @pl.when(c == n - 1)
    def _():
        sfin_ref[...] = s_scr[...]


def _ssd_scan(rev, xa, small, dtb, alog, s0, extra=None):
    q = SSD_CHUNK
    blk = lambda t: (_row_blk(t, q, rev), 0)
    st_spec = pl.BlockSpec((None, SSD_DIM, SSD_STATE), lambda t: (_seq_of(t, q), 0, 0))
    in_specs = [pl.BlockSpec((q, SSD_CONV_CH), blk),
                pl.BlockSpec((q, SMALL_W), blk),
                pl.BlockSpec((1, SMALL_W), lambda t: (0, 0)),
                pl.BlockSpec((1, SMALL_W), lambda t: (0, 0)),
                st_spec]
    args = [xa, small, dtb, alog, s0]
    if rev:
        yf, h, dexp, gn = extra
        in_specs += [pl.BlockSpec((q, SSD_DIM), blk),
                     pl.BlockSpec((q, SSD_DIM), blk),
                     pl.BlockSpec((1, SSD_DIM), lambda t: (0, 0)),
                     pl.BlockSpec((1, SSD_DIM), lambda t: (0, 0))]
        args += [yf, h, dexp, gn]
    return pl.pallas_call(
        functools.partial(_ssd_kernel, rev),
        grid=(T_ALL // q,),
        in_specs=in_specs,
        out_specs=[pl.BlockSpec((q, SSD_DIM), blk), st_spec],
        out_shape=[jax.ShapeDtypeStruct((T_ALL, SSD_DIM), BF16 if rev else F32),
                   jax.ShapeDtypeStruct((N_SEQ, SSD_DIM, SSD_STATE), F32)],
        scratch_shapes=[pltpu.VMEM((SSD_DIM, SSD_STATE), F32), pltpu.VMEM((q, SSD_DIM), F32)],
        compiler_params=_cparams(("arbitrary",)),
        name="ssd_bwd" if rev else "ssd_fwd",
    )(*args)


def _gla_kernel(rev, *refs):
    if rev:
        (q_ref, k_ref, v_ref, sm_ref, wa_ref, ba_ref, s0_ref, of_ref, gate_ref, gn_ref,
         o_ref, sfin_ref, s_scr) = refs
    else:
        q_ref, k_ref, v_ref, sm_ref, wa_ref, ba_ref, s0_ref, o_ref, sfin_ref, s_scr = refs
    t = pl.program_id(0)
    _, c, n, _ = _seq_info(t, GLA_CHUNK)
    cq = GLA_CHUNK

    @pl.when(c == 0)
    def _():
        for hh in range(GLA_HEADS):
            s_scr[hh] = s0_ref[hh].T

    ii = lax.broadcasted_iota(jnp.int32, (cq, cq), 0)
    jj = lax.broadcasted_iota(jnp.int32, (cq, cq), 1)
    mask = (jj >= ii) if rev else (jj <= ii)
    tri = mask.astype(BF16)

    g_lin = _dot(sm_ref[...].astype(BF16), wa_ref[...]) + ba_ref[...]
    g = -_softplus(-g_lin) / GLA_GATE_NORM
    cum = _mask_dot(tri, g)
    last = cum[0:1, :] if rev else cum[cq - 1:cq, :]
    q_e = (q_ref[...] * (GLA_HEAD_K ** -0.5)) * jnp.exp(cum)
    k = k_ref[...]
    k_e = k * jnp.exp(-cum)
    k_end = k * jnp.exp(last - cum)
    e_last = jnp.exp(last)

    for hh in range(GLA_HEADS):
        ks = slice(hh * GLA_HEAD_K, (hh + 1) * GLA_HEAD_K)
        vs = slice(hh * GLA_HEAD_V, (hh + 1) * GLA_HEAD_V)
        qh = q_e[:, ks].astype(BF16)
        vh = v_ref[:, vs]
        vh_b = vh.astype(BF16)
        s_prev = s_scr[hh]
        att = jnp.where(mask, _dot_nt(qh, k_e[:, ks].astype(BF16)), 0.0)
        o = _dot(att.astype(BF16), vh_b) + _dot_nt(qh, s_prev.astype(BF16))
        st = _dot(vh.T.astype(BF16), k_end[:, ks].astype(BF16))
        s_scr[hh] = s_prev * e_last[:, ks] + st
        if rev:
            o = o + of_ref[:, vs]
            o = o * lax.rsqrt(jnp.mean(o * o, axis=-1, keepdims=True) + EPS) * gn_ref[...]
            o_ref[:, vs] = (o * _silu(gate_ref[:, vs])).astype(o_ref.dtype)
        else:
            o_ref[:, vs] = o

    @pl.when(c == n - 1)
    def _():
        for hh in range(GLA_HEADS):
            sfin_ref[hh] = s_scr[hh].T


def _gla_scan(rev, h, small, wa, ba, s0, extra=None):
    cq = GLA_CHUNK
    cv = SSD_DIM // GLA_V_DIM
    cqk = (SSD_DIM + 2 * GLA_V_DIM) // GLA_QK_DIM
    row = lambda t: _row_blk(t, cq, rev)
    st_spec = pl.BlockSpec((None, GLA_HEADS, GLA_HEAD_K, GLA_HEAD_V), lambda t: (_seq_of(t, cq), 0, 0, 0))
    in_specs = [pl.BlockSpec((cq, GLA_QK_DIM), lambda t: (row(t), cqk)),
                pl.BlockSpec((cq, GLA_QK_DIM), lambda t: (row(t), cqk + 1)),
                pl.BlockSpec((cq, GLA_V_DIM), lambda t: (row(t), cv)),
                pl.BlockSpec((cq, SMALL_W), lambda t: (row(t), 0)),
                pl.BlockSpec((SMALL_W, GLA_QK_DIM), lambda t: (0, 0)),
                pl.BlockSpec((1, GLA_QK_DIM), lambda t: (0, 0)),
                st_spec]
    args = [h, h, h, small, wa, ba, s0]
    if rev:
        o_f, gn = extra
        in_specs += [pl.BlockSpec((cq, GLA_V_DIM), lambda t: (row(t), 0)),
                     pl.BlockSpec((cq, GLA_V_DIM), lambda t: (row(t), cv + 1)),
                     pl.BlockSpec((1, GLA_HEAD_V), lambda t: (0, 0))]
        args += [o_f, h, gn]
    return pl.pallas_call(
        functools.partial(_gla_kernel, rev),
        grid=(T_ALL // cq,),
        in_specs=in_specs,
        out_specs=[pl.BlockSpec((cq, GLA_V_DIM), lambda t: (row(t), 0)), st_spec],
        out_shape=[jax.ShapeDtypeStruct((T_ALL, GLA_V_DIM), BF16 if rev else F32),
                   jax.ShapeDtypeStruct((N_SEQ, GLA_HEADS, GLA_HEAD_K, GLA_HEAD_V), F32)],
        scratch_shapes=[pltpu.VMEM((GLA_HEADS, GLA_HEAD_V, GLA_HEAD_K), F32)],
        compiler_params=_cparams(("arbitrary",)),
        name="gla_bwd" if rev else "gla_fwd",
    )(*args)


ROUTER_ROWS = 32


def _first_max(vals):
    out = []
    for a, va in enumerate(vals):
        ok = None
        for b, vb in enumerate(vals):
            if a == b:
                continue
            t = (va > vb) if b < a else (va >= vb)
            ok = t if ok is None else (ok & t)
        out.append(ok)
    return out


def _router_kernel(x_ref, g_ref, sc_ref, sh_ref, whi_ref, wlo_ref, bsel_ref, tri_ref,
                   hn_ref, gates_ref, rank_ref, cnt_ref, carry):
    i = pl.program_id(0)

    @pl.when(i == 0)
    def _():
        carry[...] = jnp.zeros_like(carry)

    hn = _norm_mod(x_ref[...], g_ref[...], sc_ref[...], sh_ref[...])
    hb = hn.astype(BF16)
    hn_ref[...] = hb
    hl = (hn - hb.astype(F32)).astype(BF16)
    logits = _dot(hb, whi_ref[...]) + _dot(hb, wlo_ref[...]) + _dot(hl, whi_ref[...])
    lt = logits.T
    score = [jax.nn.sigmoid(lt[8 * m:8 * m + 8, :]) for m in range(EXPERTS_PER_GROUP)]
    sel = [score[m] + bsel_ref[8 * m:8 * m + 8, :] for m in range(EXPERTS_PER_GROUP)]

    hi1, lo1 = jnp.maximum(sel[0], sel[1]), jnp.minimum(sel[0], sel[1])
    hi2, lo2 = jnp.maximum(sel[2], sel[3]), jnp.minimum(sel[2], sel[3])
    grp = jnp.maximum(hi1, hi2) + jnp.maximum(jnp.minimum(hi1, hi2), jnp.maximum(lo1, lo2))
    best = _first_max([grp[g:g + 1, :] for g in range(N_EXPERT_GROUPS)])
    srow = lax.broadcasted_iota(jnp.int32, grp.shape, 0)
    gmask = jnp.zeros(grp.shape, jnp.bool_)
    for g in range(N_EXPERT_GROUPS):
        gmask = gmask | ((srow == g) & best[g])

    first = _first_max(sel)
    sel2 = [jnp.where(first[m], -jnp.inf, sel[m]) for m in range(EXPERTS_PER_GROUP)]
    second = _first_max(sel2)
    chosen = [(first[m] | second[m]) & gmask for m in range(EXPERTS_PER_GROUP)]
    wsum = sum(jnp.where(chosen[m], score[m], 0.0) for m in range(EXPERTS_PER_GROUP))
    wtot = jnp.sum(wsum, axis=0, keepdims=True)
    cm = jnp.concatenate([chosen[m].astype(F32) for m in range(EXPERTS_PER_GROUP)], axis=0)
    sc_all = jnp.concatenate(score, axis=0)
    gates_ref[...] = jnp.where(cm > 0, sc_all / wtot, 0.0)
    within = _dot(cm.astype(BF16), tri_ref[...])
    rank_ref[...] = jnp.where(cm > 0, carry[:, 0:1] + within, -1.0)
    carry[...] = carry[...] + jnp.sum(cm, axis=1, keepdims=True)
    cnt_ref[...] = carry[...]


def _router(x, g, sc, sh, w_router, b_router):
    slot_of_expert = np.array([(e % EXPERTS_PER_GROUP) * 8 + e // EXPERTS_PER_GROUP for e in range(N_EXPERTS)])
    w128 = jnp.zeros((D_MODEL, 128), F32).at[:, slot_of_expert].set(w_router)
    w_hi = w128.astype(BF16)
    w_lo = (w128 - w_hi.astype(F32)).astype(BF16)
    b32 = jnp.zeros((ROUTER_ROWS,), F32).at[slot_of_expert].set(b_router.astype(F32))
    bsel = jnp.broadcast_to(b32[:, None], (ROUTER_ROWS, TM))
    tri = jnp.asarray(np.triu(np.ones((TM, TM), np.float32), 1), BF16)
    mod_spec = pl.BlockSpec((None, 1, D_MODEL), lambda i: (_mod_row(i, TM), 0, 0))
    hn, gates, rank, cnt = pl.pallas_call(
        _router_kernel,
        grid=(T_ALL // TM,),
        in_specs=[pl.BlockSpec((TM, D_MODEL), lambda i: (i, 0)),
                  pl.BlockSpec((1, D_MODEL), lambda i: (0, 0)),
                  mod_spec, mod_spec,
                  pl.BlockSpec((D_MODEL, 128), lambda i: (0, 0)),
                  pl.BlockSpec((D_MODEL, 128), lambda i: (0, 0)),
                  pl.BlockSpec((ROUTER_ROWS, TM), lambda i: (0, 0)),
                  pl.BlockSpec((TM, TM), lambda i: (0, 0))],
        out_specs=[pl.BlockSpec((TM, D_MODEL), lambda i: (i, 0)),
                   pl.BlockSpec((ROUTER_ROWS, TM), lambda i: (0, i)),
                   pl.BlockSpec((ROUTER_ROWS, TM), lambda i: (0, i)),
                   pl.BlockSpec((ROUTER_ROWS, 128), lambda i: (0, 0))],
        out_shape=[jax.ShapeDtypeStruct((T_ALL, D_MODEL), BF16),
                   jax.ShapeDtypeStruct((ROUTER_ROWS, T_ALL), F32),
                   jax.ShapeDtypeStruct((ROUTER_ROWS, T_ALL), F32),
                   jax.ShapeDtypeStruct((ROUTER_ROWS, 128), F32)],
        scratch_shapes=[pltpu.VMEM((ROUTER_ROWS, 128), F32)],
        compiler_params=_cparams(("arbitrary",)),
        name="router",
    )(x, g.reshape(1, D_MODEL), sc, sh, w_hi, w_lo, bsel, tri)
    return hn, gates[slot_of_expert], rank[slot_of_expert], cnt[slot_of_expert, 0]


def _dispatch_plan(gates, rank, cnt):
    cnt = cnt.astype(jnp.int32)
    padded = ((cnt + MOE_TM - 1) // MOE_TM) * MOE_TM
    end = jnp.cumsum(padded)
    start = end - padded
    chosen = rank >= 0
    slot = start[:, None] + rank.astype(jnp.int32)
    slot_a = jnp.min(jnp.where(chosen, slot, MOE_ROWS), axis=0)
    slot_b = jnp.max(jnp.where(chosen, slot, -1), axis=0)
    w_a = jnp.sum(jnp.where(chosen & (slot == slot_a[None]), gates, 0.0), axis=0)
    w_b = jnp.sum(jnp.where(chosen & (slot == slot_b[None]), gates, 0.0), axis=0)
    tile_row = jnp.arange(MOE_TILES, dtype=jnp.int32) * MOE_TM
    tile_expert = jnp.minimum(jnp.sum(tile_row[:, None] >= end[None, :], axis=1), N_EXPERTS - 1).astype(jnp.int32)
    tile_valid = (tile_row < end[-1]).astype(jnp.int32)
    return slot_a, slot_b, w_a, w_b, tile_expert, tile_valid


def _moe_kernel(te_ref, tv_ref, x_ref, wg_ref, wu_ref, wd_ref, o_ref):
    @pl.when(tv_ref[pl.program_id(0)] > 0)
    def _():
        x = x_ref[...]
        hg = _dot(x, wg_ref[...])
        hu = _dot(x, wu_ref[...])
        act = (_silu(hg) * hu).astype(BF16)
        o_ref[...] = _dot(act, wd_ref[...])

    @pl.when(tv_ref[pl.program_id(0)] == 0)
    def _():
        o_ref[...] = jnp.zeros_like(o_ref)


def _moe_experts(tile_expert, tile_valid, xs, wg, wu, wd):
    return pl.pallas_call(
        _moe_kernel,
        grid_spec=pltpu.PrefetchScalarGridSpec(
            num_scalar_prefetch=2,
            grid=(MOE_TILES,),
            in_specs=[pl.BlockSpec((MOE_TM, D_MODEL), lambda i, te, tv: (i, 0)),
                      pl.BlockSpec((None, D_MODEL, D_FF), lambda i, te, tv: (te[i], 0, 0)),
                      pl.BlockSpec((None, D_MODEL, D_FF), lambda i, te, tv: (te[i], 0, 0)),
                      pl.BlockSpec((None, D_FF, D_MODEL), lambda i, te, tv: (te[i], 0, 0))],
            out_specs=pl.BlockSpec((MOE_TM, D_MODEL), lambda i, te, tv: (i, 0))),
        out_shape=jax.ShapeDtypeStruct((MOE_ROWS, D_MODEL), F32),
        compiler_params=_cparams(("arbitrary",)),
        name="moe_experts",
    )(tile_expert, tile_valid, xs, wg, wu, wd)


def _combine_kernel(x_ref, g_ref, ya_ref, yb_ref, wa_ref, wb_ref, o_ref):
    o_ref[...] = x_ref[...] + g_ref[...] * (wa_ref[...] * ya_ref[...] + wb_ref[...] * yb_ref[...])


def _combine(x, gate, ya, yb, w_a, w_b):
    row = pl.BlockSpec((TM, D_MODEL), lambda i: (i, 0))
    wspec = pl.BlockSpec((TM, 1), lambda i: (i, 0))
    return pl.pallas_call(
        _combine_kernel,
        grid=(T_ALL // TM,),
        in_specs=[row, pl.BlockSpec((None, 1, D_MODEL), lambda i: (_mod_row(i, TM), 0, 0)), row, row, wspec, wspec],
        out_specs=row,
        out_shape=jax.ShapeDtypeStruct((T_ALL, D_MODEL), F32),
        compiler_params=_cparams(("parallel",)),
        name="moe_combine",
    )(x, gate, ya, yb, w_a.reshape(T_ALL, 1), w_b.reshape(T_ALL, 1))


def _moe_layer(x, g, sc, sh, gate, w_router, b_router, wg, wu, wd):
    hn, gates, rank, cnt = _router(x, g, sc, sh, w_router, b_router)
    slot_a, slot_b, w_a, w_b, tile_expert, tile_valid = _dispatch_plan(gates, rank, cnt)
    tok = jnp.arange(T_ALL, dtype=jnp.int32)
    tok_of_slot = jnp.zeros((MOE_ROWS,), jnp.int32).at[slot_a].set(tok).at[slot_b].set(tok)
    xs = jnp.take(hn, tok_of_slot, axis=0)
    y = _moe_experts(tile_expert, tile_valid, xs, wg.astype(BF16), wu.astype(BF16), wd.astype(BF16))
    return _combine(x, gate, jnp.take(y, slot_a, axis=0), jnp.take(y, slot_b, axis=0), w_a, w_b)


def _final_norm_kernel(x_ref, g_ref, o_ref):
    x = x_ref[...]
    o_ref[...] = x * lax.rsqrt(jnp.mean(x * x, axis=-1, keepdims=True) + EPS) * g_ref[...]


def _final_norm(x, g):
    row = pl.BlockSpec((TM, D_MODEL), lambda i: (i, 0))
    return pl.pallas_call(
        _final_norm_kernel,
        grid=(T_ALL // TM,),
        in_specs=[row, pl.BlockSpec((1, D_MODEL), lambda i: (0, 0))],
        out_specs=row,
        out_shape=jax.ShapeDtypeStruct((T_ALL, D_MODEL), F32),
        compiler_params=_cparams(("parallel",)),
        name="final_norm",
    )(x, g.reshape(1, D_MODEL))


def _even_layer(x, mods, j, cache_kv, norm_g, w_in, w_out, sg_norm, sg_w, sg_b, rpb):
    sh1, sc1, g1 = mods
    h = _inproj(x, norm_g, sc1, sh1, w_in.astype(BF16), "inproj_even")
    a_out = _spatial_gating(h, sg_norm, sg_w, sg_b)
    o_ctx, kv = _context_attention(h)
    o_lat = _neighbourhood_attention(h, cache_kv, j, _na_bias_table(rpb))
    o = jnp.concatenate([o_ctx, o_lat], axis=0)
    return _outproj(a_out, o, w_out.astype(BF16), x, g1, "outproj_even"), kv


def _odd_layer(x, mods, state_ssm, state_gla, norm_g, w_in, w_out, conv_w, conv_b, dt_bias, a_log,
               ssd_d, ssd_norm, w_a2, b_a, gla_norm):
    sh1, sc1, g1 = mods
    o = np.cumsum([0, SSD_DIM, SSD_CONV_CH, 2 * SSD_HEADS, GLA_QK_DIM, GLA_QK_DIM, GLA_V_DIM, GLA_V_DIM, 2 * GLA_RANK])
    seg = lambda s: w_in[:, o[s]:o[s + 1]]
    w_main = jnp.concatenate([seg(0), seg(5), seg(6), seg(3), seg(4), seg(1)], axis=1).astype(BF16)
    w_small = jnp.concatenate([seg(2), seg(7), jnp.zeros((D_MODEL, SMALL_W - 2 * SSD_HEADS - 2 * GLA_RANK), F32)],
                              axis=1).astype(BF16)
    h = _inproj(x, norm_g, sc1, sh1, w_main, "inproj_odd")
    small = _inproj(x, norm_g, sc1, sh1, w_small, "inproj_odd_small")

    xa = _conv_silu(h, conv_w, conv_b)
    pad = jnp.zeros((SMALL_W - 2 * SSD_HEADS,), F32)
    dtb = jnp.concatenate([dt_bias.reshape(-1).astype(F32), pad]).reshape(1, SMALL_W)
    alog = jnp.concatenate([a_log.reshape(-1).astype(F32), pad]).reshape(1, SMALL_W)
    zeros_ssm = jnp.zeros((N_CTX_SEQ, SSD_DIM, SSD_STATE), F32)
    s0 = lambda dd: jnp.concatenate([zeros_ssm, state_ssm[:, dd].reshape(N_LAT_SEQ, SSD_DIM, SSD_STATE)], axis=0)
    y_f, sf = _ssd_scan(False, xa, small, dtb, alog, s0(0))
    dexp = jnp.repeat(ssd_d.astype(F32), SSD_HEAD_DIM).reshape(1, SSD_DIM)
    y_c, sb = _ssd_scan(True, xa, small, dtb, alog, s0(1), (y_f, h, dexp, ssd_norm.reshape(1, SSD_DIM)))
    new_ssm = jnp.stack([sf[:N_CTX_SEQ], sb[:N_CTX_SEQ]], axis=1).reshape(
        N_CTX_SEQ, 2, SSD_HEADS, SSD_HEAD_DIM, SSD_STATE)

    zeros_gla = jnp.zeros((N_CTX_SEQ, GLA_HEADS, GLA_HEAD_K, GLA_HEAD_V), F32)
    g0 = lambda dd: jnp.concatenate([zeros_gla, state_gla[:, dd]], axis=0)

    def wa_pad(dd):
        lo = 2 * SSD_HEADS + dd * GLA_RANK
        return jnp.zeros((SMALL_W, GLA_QK_DIM), F32).at[lo:lo + GLA_RANK].set(w_a2[dd]).astype(BF16)

    o_f, gf = _gla_scan(False, h, small, wa_pad(0), b_a[0].reshape(1, GLA_QK_DIM), g0(0))
    y_d, gb = _gla_scan(True, h, small, wa_pad(1), b_a[1].reshape(1, GLA_QK_DIM), g0(1),
                        (o_f, gla_norm.reshape(1, GLA_HEAD_V)))
    new_gla = jnp.stack([gf[:N_CTX_SEQ], gb[:N_CTX_SEQ]], axis=1)

    return _outproj(y_c, y_d, w_out.astype(BF16), x, g1, "outproj_odd"), new_ssm, new_gla


def kernel(x_prompt, x_sample, cache_kv, state_ssm, state_gla, c, c_ctx, w_mod, b_mod, norm_mix, norm_ffn,
           norm_final, w_in_even, w_out_even, sg_norm, sg_w, sg_b, na_rpb, w_in_odd, w_out_odd, ssd_conv_w,
           ssd_conv_b, ssd_dt_bias, ssd_a_log, ssd_d, ssd_norm, gla_w_a2, gla_b_a, gla_norm, w_router, b_router,
           w_gate, w_up, w_down):
    x = jnp.concatenate([x_prompt.reshape(T_CTX, D_MODEL), x_sample.reshape(T_LAT, D_MODEL)], axis=0)
    c_rows = jnp.concatenate([c_ctx[None, :], c, jnp.zeros((N_MOD_ROWS - 1 - N_LAT_SEQ, D_MODEL), F32)], axis=0)
    mod = _modulation(c_rows, w_mod, b_mod)

    kv_list, ssm_list, gla_list = [], [], []
    for i in range(DEPTH):
        m = mod[i, :1 + N_LAT_SEQ].reshape(1 + N_LAT_SEQ, N_MOD, 1, D_MODEL)
        sh1, sc1, g1, sh2, sc2, g2 = [m[:, s] for s in range(N_MOD)]
        j = i // 2
        if i % 2 == 0:
            x, kv = _even_layer(x, (sh1, sc1, g1), j, cache_kv, norm_mix[i], w_in_even[j], w_out_even[j],
                                sg_norm[j], sg_w[j], sg_b[j], na_rpb[j])
            kv_list.append(kv)
        else:
            x, s_ssm, s_gla = _odd_layer(x, (sh1, sc1, g1), state_ssm[:, j], state_gla[:, j], norm_mix[i],
                                         w_in_odd[j], w_out_odd[j], ssd_conv_w[j], ssd_conv_b[j],
                                         ssd_dt_bias[j], ssd_a_log[j], ssd_d[j], ssd_norm[j],
                                         gla_w_a2[j], gla_b_a[j], gla_norm[j])
            ssm_list.append(s_ssm)
            gla_list.append(s_gla)
        x = _moe_layer(x, norm_ffn[i], sc2, sh2, g2, w_router, b_router, w_gate[i], w_up[i], w_down[i])

    y = _final_norm(x, norm_final)
    return (y[:T_CTX].reshape(N_CTX_SEQ, L_CTX, D_MODEL),
            y[T_CTX:].reshape(N_LAT_SEQ, L_LAT, D_MODEL),
            jnp.stack(kv_list, axis=1),
            jnp.stack(ssm_list, axis=1),
            jnp.stack(gla_list, axis=1))
```

```python
import functools

import numpy as np
import jax
import jax.numpy as jnp
from jax import lax
from jax.experimental import pallas as pl
from jax.experimental.pallas import tpu as pltpu

F32 = jnp.float32
BF16 = jnp.bfloat16

D_MODEL = 2048
DEPTH = 4
N_EVEN = 2
N_ODD = 2
N_MOD = 6
EPS = 1e-6
NEG_INF = -1e30

N_CTX_SEQ = 16
L_CTX = 256
N_LAT_SEQ = 2
L_LAT = 2048
PAST_LEN = 512
T_CTX = N_CTX_SEQ * L_CTX
T_LAT = N_LAT_SEQ * L_LAT
T_ALL = T_CTX + T_LAT
N_MOD_ROWS = 8

GRID_W = 64
SG_DIM = 1024
SG_GROUPS = 8
SG_CHUNK = 128
NA_DIM = 1024
NA_HEADS = 8
NA_HEAD_DIM = 128
WIN_R = 8
WIN_C = 16
NA_ROWS = L_LAT // GRID_W
NA_UNROLL = 4

SSD_DIM = 2048
SSD_HEADS = 32
SSD_HEAD_DIM = 64
SSD_GROUPS = 4
SSD_GROUP_HEADS = SSD_HEADS // SSD_GROUPS
SSD_GROUP_DIM = SSD_GROUP_HEADS * SSD_HEAD_DIM
SSD_STATE = 128
SSD_CHUNK = 128
SSD_BC = SSD_GROUPS * SSD_STATE
SSD_CONV_CH = SSD_DIM + 2 * SSD_BC
GLA_HEADS = 4
GLA_QK_DIM = 1024
GLA_V_DIM = 2048
GLA_HEAD_K = 256
GLA_HEAD_V = 512
GLA_RANK = 16
GLA_GATE_NORM = 16.0
GLA_CHUNK = 64
SMALL_W = 128
ODD_COL_XBC = SSD_DIM + 2 * GLA_V_DIM + 2 * GLA_QK_DIM

N_EXPERTS = 16
N_EXPERT_GROUPS = 4
EXPERTS_PER_GROUP = 4
D_FF = 1024
MOE_TM = 256
MOE_TILES = (2 * T_ALL) // MOE_TM + N_EXPERTS
MOE_ROWS = MOE_TILES * MOE_TM

TM = 512
PROJ_TM = 1024
VMEM_LIMIT = 56 * 1024 * 1024


def _cparams(sem):
    return pltpu.CompilerParams(dimension_semantics=sem, vmem_limit_bytes=VMEM_LIMIT)


def _mod_row(i, tm):
    nct = T_CTX // tm
    per = L_LAT // tm
    return jnp.where(i < nct, 0, 1 + (i - nct) // per)


def _silu(x):
    return x * jax.nn.sigmoid(x)


def _softplus(x):
    return jnp.maximum(x, 0.0) + jnp.log1p(jnp.exp(-jnp.abs(x)))


def _gelu_tanh(x):
    return 0.5 * x * (1.0 + jnp.tanh(np.sqrt(2.0 / np.pi).astype(np.float32) * (x + 0.044715 * (x * x * x))))


def _split3(x):
    hi = x.astype(BF16)
    r1 = x - hi.astype(F32)
    mid = r1.astype(BF16)
    lo = (r1 - mid.astype(F32)).astype(BF16)
    return hi, mid, lo


def _dot(a, b):
    return jnp.dot(a, b, preferred_element_type=F32)


def _dot_nt(a, b):
    return lax.dot_general(a, b, (((1,), (1,)), ((), ())), preferred_element_type=F32)


def _mask_dot(mask_bf16, x):
    hi, mid, lo = _split3(x)
    return _dot(mask_bf16, hi) + _dot(mask_bf16, mid) + _dot(mask_bf16, lo)


def _any_spec():
    return pl.BlockSpec(memory_space=pl.ANY)


def _mod_kernel(c_ref, w_ref, b_ref, o_ref):
    s = _silu(c_ref[...]).astype(BF16)
    o_ref[...] = _dot(s, w_ref[...].astype(BF16)) + b_ref[...]


def _modulation(c_rows, w_mod, b_mod):
    tn = 1024
    n = N_MOD * D_MODEL
    return pl.pallas_call(
        _mod_kernel,
        grid=(DEPTH, n // tn),
        in_specs=[pl.BlockSpec((N_MOD_ROWS, D_MODEL), lambda l, j: (0, 0)),
                  pl.BlockSpec((None, D_MODEL, tn), lambda l, j: (l, 0, j)),
                  pl.BlockSpec((None, 1, tn), lambda l, j: (l, 0, j))],
        out_specs=pl.BlockSpec((None, N_MOD_ROWS, tn), lambda l, j: (l, 0, j)),
        out_shape=jax.ShapeDtypeStruct((DEPTH, N_MOD_ROWS, n), F32),
        compiler_params=_cparams(("parallel", "parallel")),
        name="modulation",
    )(c_rows, w_mod, b_mod.reshape(DEPTH, 1, n))


def _norm_mod(x, g, sc, sh):
    y = x * lax.rsqrt(jnp.mean(x * x, axis=-1, keepdims=True) + EPS) * g
    return y * (1.0 + sc) + sh


def _inproj_kernel(x_ref, g_ref, sc_ref, sh_ref, w_ref, o_ref, hn_ref):
    @pl.when(pl.program_id(1) == 0)
    def _():
        hn_ref[...] = _norm_mod(x_ref[...], g_ref[...], sc_ref[...], sh_ref[...]).astype(BF16)

    o_ref[...] = _dot(hn_ref[...], w_ref[...])


def _inproj(x, g, sc, sh, w, name):
    n = w.shape[1]
    tm = PROJ_TM
    tn = min(n, 1024)
    mod_spec = pl.BlockSpec((None, 1, D_MODEL), lambda i, j: (_mod_row(i, tm), 0, 0))
    return pl.pallas_call(
        _inproj_kernel,
        grid=(T_ALL // tm, n // tn),
        in_specs=[pl.BlockSpec((tm, D_MODEL), lambda i, j: (i, 0)),
                  pl.BlockSpec((1, D_MODEL), lambda i, j: (0, 0)),
                  mod_spec, mod_spec,
                  pl.BlockSpec((D_MODEL, tn), lambda i, j: (0, j))],
        out_specs=pl.BlockSpec((tm, tn), lambda i, j: (i, j)),
        out_shape=jax.ShapeDtypeStruct((T_ALL, n), F32),
        scratch_shapes=[pltpu.VMEM((tm, D_MODEL), BF16)],
        compiler_params=_cparams(("parallel", "arbitrary")),
        name=name,
    )(x, g.reshape(1, D_MODEL), sc, sh, w)


def _outproj_kernel(a_ref, b_ref, w1_ref, w2_ref, x_ref, g_ref, o_ref):
    acc = _dot(a_ref[...], w1_ref[...]) + _dot(b_ref[...], w2_ref[...])
    o_ref[...] = x_ref[...] + g_ref[...] * acc


def _outproj(a, b, w, x, gate, name):
    k1 = a.shape[1]
    tm = PROJ_TM
    tn = 512
    return pl.pallas_call(
        _outproj_kernel,
        grid=(T_ALL // tm, D_MODEL // tn),
        in_specs=[pl.BlockSpec((tm, k1), lambda i, j: (i, 0)),
                  pl.BlockSpec((tm, k1), lambda i, j: (i, 0)),
                  pl.BlockSpec((k1, tn), lambda i, j: (0, j)),
                  pl.BlockSpec((k1, tn), lambda i, j: (1, j)),
                  pl.BlockSpec((tm, tn), lambda i, j: (i, j)),
                  pl.BlockSpec((None, 1, tn), lambda i, j: (_mod_row(i, tm), 0, j))],
        out_specs=pl.BlockSpec((tm, tn), lambda i, j: (i, j)),
        out_shape=jax.ShapeDtypeStruct((T_ALL, D_MODEL), F32),
        compiler_params=_cparams(("parallel", "parallel")),
        name=name,
    )(a, b, w, w, x, gate)


def _sgate_kernel(u_ref, v_ref, g_ref, ws_ref, bs_ref, o_ref):
    u = _gelu_tanh(u_ref[...])
    v = _gelu_tanh(v_ref[...])
    v = v * lax.rsqrt(jnp.mean(v * v, axis=-1, keepdims=True) + EPS) * g_ref[...]
    vb = v.astype(BF16)
    for c in range(TM // SG_CHUNK):
        rows = slice(c * SG_CHUNK, (c + 1) * SG_CHUNK)
        for g in range(SG_GROUPS):
            cols = slice(g * 128, (g + 1) * 128)
            s = _dot(ws_ref[g], vb[rows, cols]) + bs_ref[g]
            o_ref[rows, cols] = (u[rows, cols] * s).astype(o_ref.dtype)


def _spatial_gating(h, sg_norm, sg_w, sg_b):
    bias = jnp.broadcast_to(sg_b[:, :, None], (SG_GROUPS, SG_CHUNK, 128))
    return pl.pallas_call(
        _sgate_kernel,
        grid=(T_ALL // TM,),
        in_specs=[pl.BlockSpec((TM, SG_DIM), lambda i: (i, 0)),
                  pl.BlockSpec((TM, SG_DIM), lambda i: (i, 1)),
                  pl.BlockSpec((1, SG_DIM), lambda i: (0, 0)),
                  pl.BlockSpec((SG_GROUPS, SG_CHUNK, SG_CHUNK), lambda i: (0, 0, 0)),
                  pl.BlockSpec((SG_GROUPS, SG_CHUNK, 128), lambda i: (0, 0, 0))],
        out_specs=pl.BlockSpec((TM, SG_DIM), lambda i: (i, 0)),
        out_shape=jax.ShapeDtypeStruct((T_ALL, SG_DIM), BF16),
        compiler_params=_cparams(("parallel",)),
        name="spatial_gating",
    )(h, h, sg_norm.reshape(1, SG_DIM), sg_w.astype(BF16), bias)


def _softmax_rows(s):
    e = jnp.exp(s - jnp.max(s, axis=-1, keepdims=True))
    return e / jnp.sum(e, axis=-1, keepdims=True)


def _ctx_attn_kernel(*refs):
    q_ref, k_ref, v_ref = refs[:3]
    o_ref, kv_ref = refs[-2:]
    k = k_ref[...]
    v = v_ref[...]
    s = _dot_nt(q_ref[...].astype(BF16), k.astype(BF16)) * (NA_HEAD_DIM ** -0.5)
    p = _softmax_rows(s).astype(BF16)
    o_ref[...] = _dot(p, v.astype(BF16)).astype(o_ref.dtype)
    kv_ref[0, 0] = k
    kv_ref[1, 0] = v


def _context_attention(h, j, kv_prev):
    qc = 2 * SG_DIM // NA_HEAD_DIM
    in_specs = [pl.BlockSpec((L_CTX, NA_HEAD_DIM), lambda b, hh: (b, qc + hh)),
                pl.BlockSpec((L_CTX, NA_HEAD_DIM), lambda b, hh: (b, qc + NA_HEADS + hh)),
                pl.BlockSpec((L_CTX, NA_HEAD_DIM), lambda b, hh: (b, qc + 2 * NA_HEADS + hh))]
    args = [h, h, h]
    aliases = {}
    if kv_prev is not None:
        in_specs.append(_any_spec())
        args.append(kv_prev)
        aliases = {3: 1}
    return pl.pallas_call(
        _ctx_attn_kernel,
        grid=(N_CTX_SEQ, NA_HEADS),
        in_specs=in_specs,
        out_specs=[pl.BlockSpec((L_CTX, NA_HEAD_DIM), lambda b, hh: (b, hh)),
                   pl.BlockSpec((None, None, 2, 1, L_CTX, NA_HEAD_DIM), lambda b, hh: (b, j, 0, hh, 0, 0))],
        out_shape=[jax.ShapeDtypeStruct((T_ALL, NA_DIM), BF16),
                   jax.ShapeDtypeStruct((N_CTX_SEQ, N_EVEN, 2, NA_HEADS, L_CTX, NA_HEAD_DIM), F32)],
        input_output_aliases=aliases,
        compiler_params=_cparams(("parallel", "parallel")),
        name="context_attention",
    )(*args)


def _na_bias_table(rpb):
    col = np.arange(GRID_W)
    c0 = np.clip(col - WIN_C // 2, 0, GRID_W - WIN_C)
    col_ok = (col[None, :] >= c0[:, None]) & (col[None, :] < c0[:, None] + WIN_C)
    dc = np.clip(col[None, :] - col[:, None], -(WIN_C - 1), WIN_C - 1) + (WIN_C - 1)
    dr = np.arange(WIN_R)[:, None] + np.arange(WIN_R)[None, :]
    b = rpb.astype(F32)[:, dr][..., dc]
    b = jnp.where(col_ok[None, None, None], b, NEG_INF)
    return b.transpose(0, 1, 3, 2, 4).reshape(NA_HEADS, WIN_R, GRID_W, WIN_R * GRID_W)


def _na_attn_kernel(q_ref, k_ref, v_ref, kc_ref, vc_ref, bias_ref, o_in_ref, o_ref, kb, vb, kcb, vcb):
    del o_in_ref
    kb[...] = k_ref[...].astype(BF16)
    vb[...] = v_ref[...].astype(BF16)
    kcb[...] = kc_ref[...].astype(BF16)
    vcb[...] = vc_ref[...].astype(BF16)
    scale = NA_HEAD_DIM ** -0.5
    win = WIN_R * GRID_W

    def row(r, carry):
        r0 = jnp.clip(r - WIN_R // 2, 0, NA_ROWS - WIN_R)
        d0 = r0 - r + (WIN_R - 1)
        qs = pl.multiple_of(r * GRID_W, GRID_W)
        ks = pl.multiple_of(r0 * GRID_W, GRID_W)
        q = q_ref[pl.ds(qs, GRID_W), :].astype(BF16)
        s_loc = _dot_nt(q, kb[pl.ds(ks, win), :]) * scale + bias_ref[d0]
        s_ctx = _dot_nt(q, kcb[...]) * scale
        m = jnp.maximum(jnp.max(s_loc, axis=-1, keepdims=True), jnp.max(s_ctx, axis=-1, keepdims=True))
        e_loc = jnp.exp(s_loc - m)
        e_ctx = jnp.exp(s_ctx - m)
        den = jnp.sum(e_loc, axis=-1, keepdims=True) + jnp.sum(e_ctx, axis=-1, keepdims=True)
        o = _dot((e_loc / den).astype(BF16), vb[pl.ds(ks, win), :]) + _dot((e_ctx / den).astype(BF16), vcb[...])
        o_ref[pl.ds(qs, GRID_W), :] = o.astype(o_ref.dtype)
        return carry

    lax.fori_loop(0, NA_ROWS, row, 0, unroll=NA_UNROLL)


def _neighbourhood_attention(h, cache_kv, j, bias, o_prev):
    qc = 2 * SG_DIM // NA_HEAD_DIM
    rb = T_CTX // L_LAT
    cache_spec = lambda kv: pl.BlockSpec((None, None, None, None, PAST_LEN, NA_HEAD_DIM),
                                         lambda b, hh: (b, j, kv, hh, 0, 0))
    return pl.pallas_call(
        _na_attn_kernel,
        grid=(N_LAT_SEQ, NA_HEADS),
        in_specs=[pl.BlockSpec((L_LAT, NA_HEAD_DIM), lambda b, hh: (rb + b, qc + hh)),
                  pl.BlockSpec((L_LAT, NA_HEAD_DIM), lambda b, hh: (rb + b, qc + NA_HEADS + hh)),
                  pl.BlockSpec((L_LAT, NA_HEAD_DIM), lambda b, hh: (rb + b, qc + 2 * NA_HEADS + hh)),
                  cache_spec(0), cache_spec(1),
                  pl.BlockSpec((None, WIN_R, GRID_W, WIN_R * GRID_W), lambda b, hh: (hh, 0, 0, 0)),
                  _any_spec()],
        out_specs=pl.BlockSpec((L_LAT, NA_HEAD_DIM), lambda b, hh: (rb + b, hh)),
        out_shape=jax.ShapeDtypeStruct((T_ALL, NA_DIM), BF16),
        input_output_aliases={6: 0},
        scratch_shapes=[pltpu.VMEM((L_LAT, NA_HEAD_DIM), BF16), pltpu.VMEM((L_LAT, NA_HEAD_DIM), BF16),
                        pltpu.VMEM((PAST_LEN, NA_HEAD_DIM), BF16), pltpu.VMEM((PAST_LEN, NA_HEAD_DIM), BF16)],
        compiler_params=_cparams(("parallel", "parallel")),
        name="neighbourhood_attention",
    )(h, h, h, cache_kv, cache_kv, bias, o_prev)


CONV_ROWS = 256
CONV_COLS = 512


def _conv_kernel(transpose_out, x_ref, p_ref, n_ref, w_ref, b_ref, o_ref):
    i = pl.program_id(0)
    per = L_LAT // CONV_ROWS
    nct = T_CTX // CONV_ROWS
    pos = jnp.where(i < nct, 0, (i - nct) % per)
    last = jnp.where(i < nct, 0, per - 1)
    x = x_ref[...]
    prev_row = jnp.where(pos == 0, 0.0, p_ref[7:8, :])
    next_row = jnp.where(pos == last, 0.0, n_ref[0:1, :])
    ridx = lax.broadcasted_iota(jnp.int32, x.shape, 0)
    x_dn = jnp.where(ridx == 0, prev_row, pltpu.roll(x, 1, 0))
    x_up = jnp.where(ridx == CONV_ROWS - 1, next_row, pltpu.roll(x, CONV_ROWS - 1, 0))
    y = _silu(x_dn * w_ref[0:1, :] + x * w_ref[1:2, :] + x_up * w_ref[2:3, :] + b_ref[...])
    if transpose_out:
        for c in range(CONV_ROWS // SSD_CHUNK):
            o_ref[c] = y[c * SSD_CHUNK:(c + 1) * SSD_CHUNK, :].T
    else:
        o_ref[...] = y


def _conv_silu(h, conv_w, conv_b, col0, ncols, transpose_out):
    cb0 = (ODD_COL_XBC + col0) // CONV_COLS
    wb0 = col0 // CONV_COLS
    rb = CONV_ROWS // 8
    nblk8 = T_ALL // 8
    cpb = CONV_ROWS // SSD_CHUNK
    w8 = jnp.zeros((8, SSD_CONV_CH), F32).at[:3].set(conv_w)
    if transpose_out:
        out_spec = pl.BlockSpec((cpb, CONV_COLS, SSD_CHUNK), lambda i, j: (i, j, 0))
        out_shape = jax.ShapeDtypeStruct((T_ALL // SSD_CHUNK, ncols, SSD_CHUNK), F32)
    else:
        out_spec = pl.BlockSpec((CONV_ROWS, CONV_COLS), lambda i, j: (i, j))
        out_shape = jax.ShapeDtypeStruct((T_ALL, ncols), F32)
    return pl.pallas_call(
        functools.partial(_conv_kernel, transpose_out),
        grid=(T_ALL // CONV_ROWS, ncols // CONV_COLS),
        in_specs=[pl.BlockSpec((CONV_ROWS, CONV_COLS), lambda i, j: (i, cb0 + j)),
                  pl.BlockSpec((8, CONV_COLS), lambda i, j: (jnp.maximum(i * rb - 1, 0), cb0 + j)),
                  pl.BlockSpec((8, CONV_COLS), lambda i, j: (jnp.minimum((i + 1) * rb, nblk8 - 1), cb0 + j)),
                  pl.BlockSpec((8, CONV_COLS), lambda i, j: (0, wb0 + j)),
                  pl.BlockSpec((1, CONV_COLS), lambda i, j: (0, wb0 + j))],
        out_specs=out_spec,
        out_shape=out_shape,
        compiler_params=_cparams(("parallel", "parallel")),
        name="conv_silu_t" if transpose_out else "conv_silu",
    )(h, h, h, w8, conv_b.reshape(1, SSD_CONV_CH))


def _seq_info(t, chunk):
    nct = L_CTX // chunk
    ncl = L_LAT // chunk
    nctx = N_CTX_SEQ * nct
    is_ctx = t < nctx
    tl = jnp.maximum(t - nctx, 0)
    seq = jnp.where(is_ctx, t // nct, N_CTX_SEQ + tl // ncl)
    c = jnp.where(is_ctx, t % nct, tl % ncl)
    n = jnp.where(is_ctx, nct, ncl)
    first = jnp.where(is_ctx, (t // nct) * nct, nctx + (tl // ncl) * ncl)
    return seq, c, n, first


def _row_blk(t, chunk, rev):
    if not rev:
        return t
    _, c, n, first = _seq_info(t, chunk)
    return first + (n - 1 - c)


def _lat_seq(t, chunk):
    return jnp.maximum(_seq_info(t, chunk)[0] - N_CTX_SEQ, 0)


def _ctx_seq(t, chunk):
    return jnp.minimum(_seq_info(t, chunk)[0], N_CTX_SEQ - 1)


def _head_rows(mat, d, g):
    base = d * SSD_HEADS + g * SSD_GROUP_HEADS
    return jnp.concatenate(
        [jnp.broadcast_to(mat[base + r:base + r + 1, :], (SSD_HEAD_DIM, mat.shape[1])) for r in range(SSD_GROUP_HEADS)],
        axis=0)


def _ssd_kernel(rev, has_prev, *refs):
    refs = list(refs)
    xt_ref, bc_ref, sm_ref, dtb_ref, alog_ref, s0_ref = refs[:6]
    pos = 6
    if rev:
        yf_ref, z_ref, dcol_ref, gn_ref = refs[pos:pos + 4]
        pos += 4
    if has_prev:
        pos += 1
    y_ref, sfin_ref, s_scr = refs[pos:pos + 3]
    yt_scr = refs[pos + 3] if rev else None

    t = pl.program_id(0)
    seq, c, n, _ = _seq_info(t, SSD_CHUNK)
    is_ctx = seq < N_CTX_SEQ
    q = SSD_CHUNK
    d = 1 if rev else 0

    @pl.when(c == 0)
    def _():
        s_scr[...] = jnp.where(is_ctx, 0.0, s0_ref[...])

    ii = lax.broadcasted_iota(jnp.int32, (q, q), 0)
    jj = lax.broadcasted_iota(jnp.int32, (q, q), 1)
    mask = (jj >= ii) if rev else (jj <= ii)
    tri = mask.astype(BF16)

    dt = _softplus(sm_ref[...] + dtb_ref[...])
    dta = dt * (-jnp.exp(alog_ref[...]))
    cum = _mask_dot(tri, dta)
    tot = cum[0:1, :] if rev else cum[q - 1:q, :]
    cum_t = cum.T
    dt_t = dt.T
    te_t = (jnp.exp(tot - cum) * dt).T
    ecum_t = jnp.exp(cum_t)
    last = 0 if rev else q - 1
    etot_t = jnp.broadcast_to(ecum_t[:, last:last + 1], (q, q))

    for g in range(SSD_GROUPS):
        gs = slice(g * SSD_GROUP_DIM, (g + 1) * SSD_GROUP_DIM)
        bg = bc_ref[:, g * SSD_STATE:(g + 1) * SSD_STATE].astype(BF16)
        cg = bc_ref[:, SSD_BC + g * SSD_STATE:SSD_BC + (g + 1) * SSD_STATE].astype(BF16)
        cb = _dot_nt(cg, bg)
        s_g = s_scr[gs, :]
        x_g = xt_ref[gs, :]
        y_off = _dot_nt(s_g.astype(BF16), cg) * _head_rows(ecum_t, d, g)
        st = _dot((x_g * _head_rows(te_t, d, g)).astype(BF16), bg)
        s_scr[gs, :] = s_g * _head_rows(etot_t, d, g) + st
        for k in range(SSD_GROUP_HEADS // 2):
            w_pair = []
            for r in (2 * k, 2 * k + 1):
                col = d * SSD_HEADS + g * SSD_GROUP_HEADS + r
                seg = cum[:, col:col + 1] - cum_t[col:col + 1, :]
                w_pair.append((cb * jnp.where(mask, jnp.exp(seg), 0.0) * dt_t[col:col + 1, :]).astype(BF16))
            w2 = jnp.concatenate(w_pair, axis=1)
            ps = slice(k * 2 * SSD_HEAD_DIM, (k + 1) * 2 * SSD_HEAD_DIM)
            xp = x_g[ps, :].astype(BF16)
            rr = lax.broadcasted_iota(jnp.int32, xp.shape, 0)
            zero = jnp.zeros_like(xp)
            x2 = jnp.concatenate([jnp.where(rr < SSD_HEAD_DIM, xp, zero), jnp.where(rr >= SSD_HEAD_DIM, xp, zero)],
                                 axis=1)
            y_pair = _dot_nt(x2, w2) + y_off[ps, :]
            rows = slice(g * SSD_GROUP_DIM + k * 2 * SSD_HEAD_DIM, g * SSD_GROUP_DIM + (k + 1) * 2 * SSD_HEAD_DIM)
            if rev:
                yt_scr[rows, :] = y_pair
            else:
                y_ref[rows, :] = y_pair

    if rev:
        yt = yf_ref[...] + yt_scr[...] + dcol_ref[...] * xt_ref[...]
        y = yt.T * _silu(z_ref[...])
        y = y * lax.rsqrt(jnp.mean(y * y, axis=-1, keepdims=True) + EPS) * gn_ref[...]
        y_ref[...] = y.astype(y_ref.dtype)

    @pl.when((c == n - 1) & is_ctx)
    def _():
        sfin_ref[...] = s_scr[...]


def _ssd_scan(rev, layer, xt, bc, small, dtb, alog, s0, states_prev, extra=None):
    q = SSD_CHUNK
    d = 1 if rev else 0
    blk = lambda t: _row_blk(t, q, rev)
    in_specs = [pl.BlockSpec((None, SSD_DIM, q), lambda t: (blk(t), 0, 0)),
                pl.BlockSpec((q, 2 * SSD_BC), lambda t: (blk(t), 0)),
                pl.BlockSpec((q, SMALL_W), lambda t: (blk(t), 0)),
                pl.BlockSpec((1, SMALL_W), lambda t: (0, 0)),
                pl.BlockSpec((1, SMALL_W), lambda t: (0, 0)),
                pl.BlockSpec((None, SSD_DIM, SSD_STATE), lambda t: (_lat_seq(t, q), 0, 0))]
    args = [xt, bc, small, dtb, alog, s0]
    if rev:
        yf, h, dcol, gn = extra
        in_specs += [pl.BlockSpec((None, SSD_DIM, q), lambda t: (blk(t), 0, 0)),
                     pl.BlockSpec((q, SSD_DIM), lambda t: (blk(t), 0)),
                     pl.BlockSpec((SSD_DIM, q), lambda t: (0, 0)),
                     pl.BlockSpec((1, SSD_DIM), lambda t: (0, 0))]
        args += [yf, h, dcol, gn]
        y_spec = pl.BlockSpec((q, SSD_DIM), lambda t: (blk(t), 0))
        y_shape = jax.ShapeDtypeStruct((T_ALL, SSD_DIM), BF16)
    else:
        y_spec = pl.BlockSpec((None, SSD_DIM, q), lambda t: (blk(t), 0, 0))
        y_shape = jax.ShapeDtypeStruct((T_ALL // q, SSD_DIM, q), F32)
    aliases = {}
    if states_prev is not None:
        aliases = {len(args): 1}
        in_specs.append(_any_spec())
        args.append(states_prev)
    scratch = [pltpu.VMEM((SSD_DIM, SSD_STATE), F32)]
    if rev:
        scratch.append(pltpu.VMEM((SSD_DIM, q), F32))
    return pl.pallas_call(
        functools.partial(_ssd_kernel, rev, states_prev is not None),
        grid=(T_ALL // q,),
        in_specs=in_specs,
        out_specs=[y_spec,
                   pl.BlockSpec((None, None, None, SSD_DIM, SSD_STATE), lambda t: (_ctx_seq(t, q), layer, d, 0, 0))],
        out_shape=[y_shape, jax.ShapeDtypeStruct((N_CTX_SEQ, N_ODD, 2, SSD_DIM, SSD_STATE), F32)],
        input_output_aliases=aliases,
        scratch_shapes=scratch,
        compiler_params=_cparams(("arbitrary",)),
        name="ssd_bwd" if rev else "ssd_fwd",
    )(*args)


def _gla_kernel(rev, has_prev, *refs):
    refs = list(refs)
    q_ref, k_ref, v_ref, sm_ref, wa_ref, ba_ref, s0_ref = refs[:7]
    pos = 7
    if rev:
        of_ref, gate_ref, gn_ref = refs[pos:pos + 3]
        pos += 3
    if has_prev:
        pos += 1
    o_ref, sfin_ref, s_scr = refs[pos:pos + 3]

    t = pl.program_id(0)
    seq, c, n, _ = _seq_info(t, GLA_CHUNK)
    is_ctx = seq < N_CTX_SEQ
    cq = GLA_CHUNK

    @pl.when(c == 0)
    def _():
        for hh in range(GLA_HEADS):
            s_scr[hh] = jnp.where(is_ctx, 0.0, s0_ref[hh].T)

    ii = lax.broadcasted_iota(jnp.int32, (cq, cq), 0)
    jj = lax.broadcasted_iota(jnp.int32, (cq, cq), 1)
    mask = (jj >= ii) if rev else (jj <= ii)
    tri = mask.astype(BF16)

    g_lin = _dot(sm_ref[...].astype(BF16), wa_ref[...]) + ba_ref[...]
    g = -_softplus(-g_lin) / GLA_GATE_NORM
    cum = _mask_dot(tri, g)
    last = cum[0:1, :] if rev else cum[cq - 1:cq, :]
    q_e = (q_ref[...] * (GLA_HEAD_K ** -0.5)) * jnp.exp(cum)
    k = k_ref[...]
    k_e = k * jnp.exp(-cum)
    k_end = k * jnp.exp(last - cum)
    e_last = jnp.exp(last)

    for hh in range(GLA_HEADS):
        ks = slice(hh * GLA_HEAD_K, (hh + 1) * GLA_HEAD_K)
        vs = slice(hh * GLA_HEAD_V, (hh + 1) * GLA_HEAD_V)
        qh = q_e[:, ks].astype(BF16)
        vh = v_ref[:, vs]
        vh_b = vh.astype(BF16)
        s_prev = s_scr[hh]
        att = jnp.where(mask, _dot_nt(qh, k_e[:, ks].astype(BF16)), 0.0)
        o = _dot(att.astype(BF16), vh_b) + _dot_nt(qh, s_prev.astype(BF16))
        st = _dot(vh.T.astype(BF16), k_end[:, ks].astype(BF16))
        s_scr[hh] = s_prev * e_last[:, ks] + st
        if rev:
            o = o + of_ref[:, vs]
            o = o * lax.rsqrt(jnp.mean(o * o, axis=-1, keepdims=True) + EPS) * gn_ref[...]
            o_ref[:, vs] = (o * _silu(gate_ref[:, vs])).astype(o_ref.dtype)
        else:
            o_ref[:, vs] = o

    @pl.when((c == n - 1) & is_ctx)
    def _():
        for hh in range(GLA_HEADS):
            sfin_ref[hh] = s_scr[hh].T


def _gla_scan(rev, layer, h, small, wa, ba, s0, states_prev, extra=None):
    cq = GLA_CHUNK
    d = 1 if rev else 0
    cv = SSD_DIM // GLA_V_DIM
    cqk = (SSD_DIM + 2 * GLA_V_DIM) // GLA_QK_DIM
    row = lambda t: _row_blk(t, cq, rev)
    in_specs = [pl.BlockSpec((cq, GLA_QK_DIM), lambda t: (row(t), cqk)),
                pl.BlockSpec((cq, GLA_QK_DIM), lambda t: (row(t), cqk + 1)),
                pl.BlockSpec((cq, GLA_V_DIM), lambda t: (row(t), cv)),
                pl.BlockSpec((cq, SMALL_W), lambda t: (row(t), 0)),
                pl.BlockSpec((SMALL_W, GLA_QK_DIM), lambda t: (0, 0)),
                pl.BlockSpec((1, GLA_QK_DIM), lambda t: (0, 0)),
                pl.BlockSpec((None, GLA_HEADS, GLA_HEAD_K, GLA_HEAD_V), lambda t: (_lat_seq(t, cq), 0, 0, 0))]
    args = [h, h, h, small, wa, ba, s0]
    if rev:
        o_f, gn = extra
        in_specs += [pl.BlockSpec((cq, GLA_V_DIM), lambda t: (row(t), 0)),
                     pl.BlockSpec((cq, GLA_V_DIM), lambda t: (row(t), cv + 1)),
                     pl.BlockSpec((1, GLA_HEAD_V), lambda t: (0, 0))]
        args += [o_f, h, gn]
    aliases = {}
    if states_prev is not None:
        aliases = {len(args): 1}
        in_specs.append(_any_spec())
        args.append(states_prev)
    return pl.pallas_call(
        functools.partial(_gla_kernel, rev, states_prev is not None),
        grid=(T_ALL // cq,),
        in_specs=in_specs,
        out_specs=[pl.BlockSpec((cq, GLA_V_DIM), lambda t: (row(t), 0)),
                   pl.BlockSpec((None, None, None, GLA_HEADS, GLA_HEAD_K, GLA_HEAD_V),
                                lambda t: (_ctx_seq(t, cq), layer, d, 0, 0, 0))],
        out_shape=[jax.ShapeDtypeStruct((T_ALL, GLA_V_DIM), BF16 if rev else F32),
                   jax.ShapeDtypeStruct((N_CTX_SEQ, N_ODD, 2, GLA_HEADS, GLA_HEAD_K, GLA_HEAD_V), F32)],
        input_output_aliases=aliases,
        scratch_shapes=[pltpu.VMEM((GLA_HEADS, GLA_HEAD_V, GLA_HEAD_K), F32)],
        compiler_params=_cparams(("arbitrary",)),
        name="gla_bwd" if rev else "gla_fwd",
    )(*args)


ROUTER_ROWS = 32


def _first_max(vals):
    out = []
    for a, va in enumerate(vals):
        ok = None
        for b, vb in enumerate(vals):
            if a == b:
                continue
            t = (va > vb) if b < a else (va >= vb)
            ok = t if ok is None else (ok & t)
        out.append(ok)
    return out


def _router_kernel(x_ref, g_ref, sc_ref, sh_ref, whi_ref, wlo_ref, bsel_ref, tri_ref,
                   hn_ref, gates_ref, rank_ref, cnt_ref, carry):
    i = pl.program_id(0)

    @pl.when(i == 0)
    def _():
        carry[...] = jnp.zeros_like(carry)

    hn = _norm_mod(x_ref[...], g_ref[...], sc_ref[...], sh_ref[...])
    hn_ref[...] = hn
    hb = hn.astype(BF16)
    hl = (hn - hb.astype(F32)).astype(BF16)
    logits = _dot(hb, whi_ref[...]) + _dot(hb, wlo_ref[...]) + _dot(hl, whi_ref[...])
    lt = logits.T
    score = [jax.nn.sigmoid(lt[8 * m:8 * m + 8, :]) for m in range(EXPERTS_PER_GROUP)]
    sel = [score[m] + bsel_ref[8 * m:8 * m + 8, :] for m in range(EXPERTS_PER_GROUP)]

    hi1, lo1 = jnp.maximum(sel[0], sel[1]), jnp.minimum(sel[0], sel[1])
    hi2, lo2 = jnp.maximum(sel[2], sel[3]), jnp.minimum(sel[2], sel[3])
    grp = jnp.maximum(hi1, hi2) + jnp.maximum(jnp.minimum(hi1, hi2), jnp.maximum(lo1, lo2))
    best = _first_max([grp[g:g + 1, :] for g in range(N_EXPERT_GROUPS)])
    srow = lax.broadcasted_iota(jnp.int32, grp.shape, 0)
    gmask = jnp.zeros(grp.shape, jnp.bool_)
    for g in range(N_EXPERT_GROUPS):
        gmask = gmask | ((srow == g) & best[g])

    first = _first_max(sel)
    sel2 = [jnp.where(first[m], -jnp.inf, sel[m]) for m in range(EXPERTS_PER_GROUP)]
    second = _first_max(sel2)
    chosen = [(first[m] | second[m]) & gmask for m in range(EXPERTS_PER_GROUP)]
    wsum = sum(jnp.where(chosen[m], score[m], 0.0) for m in range(EXPERTS_PER_GROUP))
    wtot = jnp.sum(wsum, axis=0, keepdims=True)
    cm = jnp.concatenate([chosen[m].astype(F32) for m in range(EXPERTS_PER_GROUP)], axis=0)
    sc_all = jnp.concatenate(score, axis=0)
    gates_ref[...] = jnp.where(cm > 0, sc_all / wtot, 0.0)
    within = _dot(cm.astype(BF16), tri_ref[...])
    rank_ref[...] = jnp.where(cm > 0, carry[:, 0:1] + within, -1.0)
    carry[...] = carry[...] + jnp.sum(cm, axis=1, keepdims=True)
    cnt_ref[...] = carry[...]


def _router(x, g, sc, sh, w_router, b_router):
    slot_of_expert = np.array([(e % EXPERTS_PER_GROUP) * 8 + e // EXPERTS_PER_GROUP for e in range(N_EXPERTS)])
    w128 = jnp.zeros((D_MODEL, 128), F32).at[:, slot_of_expert].set(w_router)
    w_hi = w128.astype(BF16)
    w_lo = (w128 - w_hi.astype(F32)).astype(BF16)
    b32 = jnp.zeros((ROUTER_ROWS,), F32).at[slot_of_expert].set(b_router.astype(F32))
    bsel = jnp.broadcast_to(b32[:, None], (ROUTER_ROWS, TM))
    tri = jnp.asarray(np.triu(np.ones((TM, TM), np.float32), 1), BF16)
    mod_spec = pl.BlockSpec((None, 1, D_MODEL), lambda i: (_mod_row(i, TM), 0, 0))
    hn, gates, rank, cnt = pl.pallas_call(
        _router_kernel,
        grid=(T_ALL // TM,),
        in_specs=[pl.BlockSpec((TM, D_MODEL), lambda i: (i, 0)),
                  pl.BlockSpec((1, D_MODEL), lambda i: (0, 0)),
                  mod_spec, mod_spec,
                  pl.BlockSpec((D_MODEL, 128), lambda i: (0, 0)),
                  pl.BlockSpec((D_MODEL, 128), lambda i: (0, 0)),
                  pl.BlockSpec((ROUTER_ROWS, TM), lambda i: (0, 0)),
                  pl.BlockSpec((TM, TM), lambda i: (0, 0))],
        out_specs=[pl.BlockSpec((TM, D_MODEL), lambda i: (i, 0)),
                   pl.BlockSpec((ROUTER_ROWS, TM), lambda i: (0, i)),
                   pl.BlockSpec((ROUTER_ROWS, TM), lambda i: (0, i)),
                   pl.BlockSpec((ROUTER_ROWS, 128), lambda i: (0, 0))],
        out_shape=[jax.ShapeDtypeStruct((T_ALL, D_MODEL), F32),
                   jax.ShapeDtypeStruct((ROUTER_ROWS, T_ALL), F32),
                   jax.ShapeDtypeStruct((ROUTER_ROWS, T_ALL), F32),
                   jax.ShapeDtypeStruct((ROUTER_ROWS, 128), F32)],
        scratch_shapes=[pltpu.VMEM((ROUTER_ROWS, 128), F32)],
        compiler_params=_cparams(("arbitrary",)),
        name="router",
    )(x, g.reshape(1, D_MODEL), sc, sh, w_hi, w_lo, bsel, tri)
    return hn, gates[slot_of_expert], rank[slot_of_expert], cnt[slot_of_expert, 0]


def _dispatch_plan(gates, rank, cnt):
    cnt = cnt.astype(jnp.int32)
    padded = ((cnt + MOE_TM - 1) // MOE_TM) * MOE_TM
    end = jnp.cumsum(padded)
    start = end - padded
    chosen = rank >= 0
    slot = start[:, None] + rank.astype(jnp.int32)
    slot_a = jnp.min(jnp.where(chosen, slot, MOE_ROWS), axis=0)
    slot_b = jnp.max(jnp.where(chosen, slot, -1), axis=0)
    w_a = jnp.sum(jnp.where(chosen & (slot == slot_a[None]), gates, 0.0), axis=0)
    w_b = jnp.sum(jnp.where(chosen & (slot == slot_b[None]), gates, 0.0), axis=0)
    tile_row = jnp.arange(MOE_TILES, dtype=jnp.int32) * MOE_TM
    tile_expert = jnp.minimum(jnp.sum(tile_row[:, None] >= end[None, :], axis=1), N_EXPERTS - 1).astype(jnp.int32)
    tile_valid = (tile_row < end[-1]).astype(jnp.int32)
    return slot_a, slot_b, w_a, w_b, tile_expert, tile_valid


def _moe_kernel(te_ref, tv_ref, x_ref, wg_ref, wu_ref, wd_ref, o_ref):
    @pl.when(tv_ref[pl.program_id(0)] > 0)
    def _():
        x = x_ref[...].astype(BF16)
        hg = _dot(x, wg_ref[...])
        hu = _dot(x, wu_ref[...])
        act = (_silu(hg) * hu).astype(BF16)
        o_ref[...] = _dot(act, wd_ref[...])

    @pl.when(tv_ref[pl.program_id(0)] == 0)
    def _():
        o_ref[...] = jnp.zeros_like(o_ref)


def _moe_experts(layer, tile_expert, tile_valid, xs, wg, wu, wd):
    e0 = layer * N_EXPERTS
    return pl.pallas_call(
        _moe_kernel,
        grid_spec=pltpu.PrefetchScalarGridSpec(
            num_scalar_prefetch=2,
            grid=(MOE_TILES,),
            in_specs=[pl.BlockSpec((MOE_TM, D_MODEL), lambda i, te, tv: (i, 0)),
                      pl.BlockSpec((None, D_MODEL, D_FF), lambda i, te, tv: (e0 + te[i], 0, 0)),
                      pl.BlockSpec((None, D_MODEL, D_FF), lambda i, te, tv: (e0 + te[i], 0, 0)),
                      pl.BlockSpec((None, D_FF, D_MODEL), lambda i, te, tv: (e0 + te[i], 0, 0))],
            out_specs=pl.BlockSpec((MOE_TM, D_MODEL), lambda i, te, tv: (i, 0))),
        out_shape=jax.ShapeDtypeStruct((MOE_ROWS, D_MODEL), F32),
        compiler_params=_cparams(("arbitrary",)),
        name="moe_experts",
    )(tile_expert, tile_valid, xs, wg, wu, wd)


def _combine_kernel(x_ref, g_ref, ya_ref, yb_ref, wa_ref, wb_ref, o_ref):
    o_ref[...] = x_ref[...] + g_ref[...] * (wa_ref[...] * ya_ref[...] + wb_ref[...] * yb_ref[...])


def _combine(x, gate, ya, yb, w_a, w_b):
    row = pl.BlockSpec((TM, D_MODEL), lambda i: (i, 0))
    wspec = pl.BlockSpec((TM, 1), lambda i: (i, 0))
    return pl.pallas_call(
        _combine_kernel,
        grid=(T_ALL // TM,),
        in_specs=[row, pl.BlockSpec((None, 1, D_MODEL), lambda i: (_mod_row(i, TM), 0, 0)), row, row, wspec, wspec],
        out_specs=row,
        out_shape=jax.ShapeDtypeStruct((T_ALL, D_MODEL), F32),
        compiler_params=_cparams(("parallel",)),
        name="moe_combine",
    )(x, gate, ya, yb, w_a.reshape(T_ALL, 1), w_b.reshape(T_ALL, 1))


def _take_rows(a, idx):
    return a.at[idx].get(mode="promise_in_bounds")


def _moe_layer(layer, x, g, sc, sh, gate, w_router, b_router, wg, wu, wd):
    hn, gates, rank, cnt = _router(x, g, sc, sh, w_router, b_router)
    slot_a, slot_b, w_a, w_b, tile_expert, tile_valid = _dispatch_plan(gates, rank, cnt)
    tok = jnp.arange(T_ALL, dtype=jnp.int32)
    tok_of_slot = jnp.zeros((MOE_ROWS,), jnp.int32).at[slot_a].set(tok).at[slot_b].set(tok)
    xs = _take_rows(hn, tok_of_slot)
    y = _moe_experts(layer, tile_expert, tile_valid, xs, wg, wu, wd)
    return _combine(x, gate, _take_rows(y, slot_a), _take_rows(y, slot_b), w_a, w_b)


def _final_norm_kernel(x_ref, g_ref, o_ref):
    x = x_ref[...]
    o_ref[...] = x * lax.rsqrt(jnp.mean(x * x, axis=-1, keepdims=True) + EPS) * g_ref[...]


def _final_norm(x, g):
    row = pl.BlockSpec((TM, D_MODEL), lambda i: (i, 0))
    return pl.pallas_call(
        _final_norm_kernel,
        grid=(T_ALL // TM,),
        in_specs=[row, pl.BlockSpec((1, D_MODEL), lambda i: (0, 0))],
        out_specs=row,
        out_shape=jax.ShapeDtypeStruct((T_ALL, D_MODEL), F32),
        compiler_params=_cparams(("parallel",)),
        name="final_norm",
    )(x, g.reshape(1, D_MODEL))


def _even_layer(x, mods, j, cache_kv, kv_new, norm_g, w_in, w_out, sg_norm, sg_w, sg_b, rpb):
    sh1, sc1, g1 = mods
    h = _inproj(x, norm_g, sc1, sh1, w_in.astype(BF16), "inproj_even")
    a_out = _spatial_gating(h, sg_norm, sg_w, sg_b)
    o, kv_new = _context_attention(h, j, kv_new)
    o = _neighbourhood_attention(h, cache_kv, j, _na_bias_table(rpb), o)
    return _outproj(a_out, o, w_out.astype(BF16), x, g1, "outproj_even"), kv_new


def _odd_layer(x, mods, j, state_ssm, state_gla, ssm_new, gla_new, norm_g, w_in, w_out, conv_w, conv_b, dt_bias,
               a_log, ssd_d, ssd_norm, w_a2, b_a, gla_norm):
    sh1, sc1, g1 = mods
    o = np.cumsum([0, SSD_DIM, SSD_CONV_CH, 2 * SSD_HEADS, GLA_QK_DIM, GLA_QK_DIM, GLA_V_DIM, GLA_V_DIM, 2 * GLA_RANK])
    seg = lambda s: w_in[:, o[s]:o[s + 1]]
    w_main = jnp.concatenate([seg(0), seg(5), seg(6), seg(3), seg(4), seg(1)], axis=1).astype(BF16)
    w_small = jnp.concatenate([seg(2), seg(7), jnp.zeros((D_MODEL, SMALL_W - 2 * SSD_HEADS - 2 * GLA_RANK), F32)],
                              axis=1).astype(BF16)
    h = _inproj(x, norm_g, sc1, sh1, w_main, "inproj_odd")
    small = _inproj(x, norm_g, sc1, sh1, w_small, "inproj_odd_small")

    xt = _conv_silu(h, conv_w, conv_b, 0, SSD_DIM, True)
    bc = _conv_silu(h, conv_w, conv_b, SSD_DIM, 2 * SSD_BC, False)
    pad = jnp.zeros((SMALL_W - 2 * SSD_HEADS,), F32)
    dtb = jnp.concatenate([dt_bias.reshape(-1).astype(F32), pad]).reshape(1, SMALL_W)
    alog = jnp.concatenate([a_log.reshape(-1).astype(F32), pad]).reshape(1, SMALL_W)
    s0 = state_ssm.reshape(N_LAT_SEQ, 2, SSD_DIM, SSD_STATE)
    y_f, ssm_new = _ssd_scan(False, j, xt, bc, small, dtb, alog, s0[:, 0], ssm_new)
    dcol = jnp.broadcast_to(jnp.repeat(ssd_d.astype(F32), SSD_HEAD_DIM)[:, None], (SSD_DIM, SSD_CHUNK))
    y_c, ssm_new = _ssd_scan(True, j, xt, bc, small, dtb, alog, s0[:, 1], ssm_new,
                             (y_f, h, dcol, ssd_norm.reshape(1, SSD_DIM)))

    def wa_pad(dd):
        lo = 2 * SSD_HEADS + dd * GLA_RANK
        return jnp.zeros((SMALL_W, GLA_QK_DIM), F32).at[lo:lo + GLA_RANK].set(w_a2[dd]).astype(BF16)

    o_f, gla_new = _gla_scan(False, j, h, small, wa_pad(0), b_a[0].reshape(1, GLA_QK_DIM), state_gla[:, 0], gla_new)
    y_d, gla_new = _gla_scan(True, j, h, small, wa_pad(1), b_a[1].reshape(1, GLA_QK_DIM), state_gla[:, 1], gla_new,
                             (o_f, gla_norm.reshape(1, GLA_HEAD_V)))

    return _outproj(y_c, y_d, w_out.astype(BF16), x, g1, "outproj_odd"), ssm_new, gla_new


def kernel(x_prompt, x_sample, cache_kv, state_ssm, state_gla, c, c_ctx, w_mod, b_mod, norm_mix, norm_ffn,
           norm_final, w_in_even, w_out_even, sg_norm, sg_w, sg_b, na_rpb, w_in_odd, w_out_odd, ssd_conv_w,
           ssd_conv_b, ssd_dt_bias, ssd_a_log, ssd_d, ssd_norm, gla_w_a2, gla_b_a, gla_norm, w_router, b_router,
           w_gate, w_up, w_down):
    x = jnp.concatenate([x_prompt.reshape(T_CTX, D_MODEL), x_sample.reshape(T_LAT, D_MODEL)], axis=0)
    c_rows = jnp.concatenate([c_ctx[None, :], c, jnp.zeros((N_MOD_ROWS - 1 - N_LAT_SEQ, D_MODEL), F32)], axis=0)
    mod = _modulation(c_rows, w_mod, b_mod)
    wg = w_gate.astype(BF16).reshape(DEPTH * N_EXPERTS, D_MODEL, D_FF)
    wu = w_up.astype(BF16).reshape(DEPTH * N_EXPERTS, D_MODEL, D_FF)
    wd = w_down.astype(BF16).reshape(DEPTH * N_EXPERTS, D_FF, D_MODEL)

    kv_new, ssm_new, gla_new = None, None, None
    for i in range(DEPTH):
        m = mod[i, :1 + N_LAT_SEQ].reshape(1 + N_LAT_SEQ, N_MOD, 1, D_MODEL)
        sh1, sc1, g1, sh2, sc2, g2 = [m[:, s] for s in range(N_MOD)]
        j = i // 2
        if i % 2 == 0:
            x, kv_new = _even_layer(x, (sh1, sc1, g1), j, cache_kv, kv_new, norm_mix[i], w_in_even[j],
                                    w_out_even[j], sg_norm[j], sg_w[j], sg_b[j], na_rpb[j])
        else:
            x, ssm_new, gla_new = _odd_layer(x, (sh1, sc1, g1), j, state_ssm[:, j], state_gla[:, j], ssm_new,
                                             gla_new, norm_mix[i], w_in_odd[j], w_out_odd[j], ssd_conv_w[j],
                                             ssd_conv_b[j], ssd_dt_bias[j], ssd_a_log[j], ssd_d[j], ssd_norm[j],
                                             gla_w_a2[j], gla_b_a[j], gla_norm[j])
        x = _moe_layer(i, x, norm_ffn[i], sc2, sh2, g2, w_router, b_router, wg, wu, wd)

    y = _final_norm(x, norm_final)
    return (y[:T_CTX].reshape(N_CTX_SEQ, L_CTX, D_MODEL),
            y[T_CTX:].reshape(N_LAT_SEQ, L_LAT, D_MODEL),
            kv_new,
            ssm_new.reshape(N_CTX_SEQ, N_ODD, 2, SSD_HEADS, SSD_HEAD_DIM, SSD_STATE),
            gla_new)
```

```python
import functools

import numpy as np
import jax
import jax.numpy as jnp
from jax import lax
from jax.experimental import pallas as pl
from jax.experimental.pallas import tpu as pltpu

F32 = jnp.float32
BF16 = jnp.bfloat16

D_MODEL = 2048
DEPTH = 4
N_EVEN = 2
N_ODD = 2
N_MOD = 6
EPS = 1e-6
NEG_INF = -1e30

N_CTX_SEQ = 16
L_CTX = 256
N_LAT_SEQ = 2
L_LAT = 2048
PAST_LEN = 512
T_CTX = N_CTX_SEQ * L_CTX
T_LAT = N_LAT_SEQ * L_LAT
T_ALL = T_CTX + T_LAT
N_MOD_ROWS = 8

GRID_W = 64
SG_DIM = 1024
SG_GROUPS = 8
SG_CHUNK = 128
NA_DIM = 1024
NA_HEADS = 8
NA_HEAD_DIM = 128
WIN_R = 8
WIN_C = 16
NA_ROWS = L_LAT // GRID_W
NA_UNROLL = 4

SSD_DIM = 2048
SSD_HEADS = 32
SSD_HEAD_DIM = 64
SSD_GROUPS = 4
SSD_GROUP_HEADS = SSD_HEADS // SSD_GROUPS
SSD_GROUP_DIM = SSD_GROUP_HEADS * SSD_HEAD_DIM
SSD_STATE = 128
SSD_CHUNK = 128
SSD_BC = SSD_GROUPS * SSD_STATE
SSD_CONV_CH = SSD_DIM + 2 * SSD_BC
GLA_HEADS = 4
GLA_QK_DIM = 1024
GLA_V_DIM = 2048
GLA_HEAD_K = 256
GLA_HEAD_V = 512
GLA_RANK = 16
GLA_GATE_NORM = 16.0
GLA_CHUNK = 64
SMALL_W = 128
ODD_COL_XBC = SSD_DIM + 2 * GLA_V_DIM + 2 * GLA_QK_DIM

N_EXPERTS = 16
N_EXPERT_GROUPS = 4
EXPERTS_PER_GROUP = 4
D_FF = 1024
MOE_TM = 256
MOE_TILES = (2 * T_ALL) // MOE_TM + N_EXPERTS
MOE_ROWS = MOE_TILES * MOE_TM

TM = 512
PROJ_TM = 1024
VMEM_LIMIT = 56 * 1024 * 1024


def _cparams(sem):
    return pltpu.CompilerParams(dimension_semantics=sem, vmem_limit_bytes=VMEM_LIMIT)


def _mod_row(i, tm):
    nct = T_CTX // tm
    per = L_LAT // tm
    return jnp.where(i < nct, 0, 1 + (i - nct) // per)


def _silu(x):
    return x * jax.nn.sigmoid(x)


def _softplus(x):
    return jnp.maximum(x, 0.0) + jnp.log1p(jnp.exp(-jnp.abs(x)))


def _gelu_tanh(x):
    return 0.5 * x * (1.0 + jnp.tanh(np.sqrt(2.0 / np.pi).astype(np.float32) * (x + 0.044715 * (x * x * x))))


def _split3(x):
    hi = x.astype(BF16)
    r1 = x - hi.astype(F32)
    mid = r1.astype(BF16)
    lo = (r1 - mid.astype(F32)).astype(BF16)
    return hi, mid, lo


def _dot(a, b):
    return jnp.dot(a, b, preferred_element_type=F32)


def _dot_nt(a, b):
    return lax.dot_general(a, b, (((1,), (1,)), ((), ())), preferred_element_type=F32)


def _mask_dot(mask_bf16, x):
    hi, mid, lo = _split3(x)
    return _dot(mask_bf16, hi) + _dot(mask_bf16, mid) + _dot(mask_bf16, lo)


def _any_spec():
    return pl.BlockSpec(memory_space=pl.ANY)


def _mod_kernel(c_ref, w_ref, b_ref, o_ref):
    s = _silu(c_ref[...]).astype(BF16)
    o_ref[...] = _dot(s, w_ref[...].astype(BF16)) + b_ref[...]


def _modulation(c_rows, w_mod, b_mod):
    tn = 1024
    n = N_MOD * D_MODEL
    return pl.pallas_call(
        _mod_kernel,
        grid=(DEPTH, n // tn),
        in_specs=[pl.BlockSpec((N_MOD_ROWS, D_MODEL), lambda l, j: (0, 0)),
                  pl.BlockSpec((None, D_MODEL, tn), lambda l, j: (l, 0, j)),
                  pl.BlockSpec((None, 1, tn), lambda l, j: (l, 0, j))],
        out_specs=pl.BlockSpec((None, N_MOD_ROWS, tn), lambda l, j: (l, 0, j)),
        out_shape=jax.ShapeDtypeStruct((DEPTH, N_MOD_ROWS, n), F32),
        compiler_params=_cparams(("parallel", "parallel")),
        name="modulation",
    )(c_rows, w_mod, b_mod.reshape(DEPTH, 1, n))


def _norm_mod(x, g, sc, sh):
    y = x * lax.rsqrt(jnp.mean(x * x, axis=-1, keepdims=True) + EPS) * g
    return y * (1.0 + sc) + sh


def _inproj_kernel(x_ref, g_ref, sc_ref, sh_ref, w_ref, o_ref, hn_ref):
    @pl.when(pl.program_id(1) == 0)
    def _():
        hn_ref[...] = _norm_mod(x_ref[...], g_ref[...], sc_ref[...], sh_ref[...]).astype(BF16)

    o_ref[...] = _dot(hn_ref[...], w_ref[...].astype(BF16))


def _inproj(x, g, sc, sh, w, name):
    n = w.shape[1]
    tm = PROJ_TM
    tn = min(n, 1024)
    mod_spec = pl.BlockSpec((None, 1, D_MODEL), lambda i, j: (_mod_row(i, tm), 0, 0))
    return pl.pallas_call(
        _inproj_kernel,
        grid=(T_ALL // tm, n // tn),
        in_specs=[pl.BlockSpec((tm, D_MODEL), lambda i, j: (i, 0), pipeline_mode=pl.Buffered(1)),
                  pl.BlockSpec((1, D_MODEL), lambda i, j: (0, 0)),
                  mod_spec, mod_spec,
                  pl.BlockSpec((D_MODEL, tn), lambda i, j: (0, j))],
        out_specs=pl.BlockSpec((tm, tn), lambda i, j: (i, j)),
        out_shape=jax.ShapeDtypeStruct((T_ALL, n), F32),
        scratch_shapes=[pltpu.VMEM((tm, D_MODEL), BF16)],
        compiler_params=_cparams(("parallel", "arbitrary")),
        name=name,
    )(x, g.reshape(1, D_MODEL), sc, sh, w)


def _outproj_kernel(a_ref, b_ref, w1_ref, w2_ref, x_ref, g_ref, o_ref):
    acc = _dot(a_ref[...], w1_ref[...].astype(BF16)) + _dot(b_ref[...], w2_ref[...].astype(BF16))
    o_ref[...] = x_ref[...] + g_ref[...] * acc


def _outproj(a, b, w, x, gate, name):
    k1 = a.shape[1]
    tm = PROJ_TM
    tn = 512
    return pl.pallas_call(
        _outproj_kernel,
        grid=(T_ALL // tm, D_MODEL // tn),
        in_specs=[pl.BlockSpec((tm, k1), lambda i, j: (i, 0)),
                  pl.BlockSpec((tm, k1), lambda i, j: (i, 0)),
                  pl.BlockSpec((k1, tn), lambda i, j: (0, j)),
                  pl.BlockSpec((k1, tn), lambda i, j: (1, j)),
                  pl.BlockSpec((tm, tn), lambda i, j: (i, j)),
                  pl.BlockSpec((None, 1, tn), lambda i, j: (_mod_row(i, tm), 0, j))],
        out_specs=pl.BlockSpec((tm, tn), lambda i, j: (i, j)),
        out_shape=jax.ShapeDtypeStruct((T_ALL, D_MODEL), F32),
        compiler_params=_cparams(("parallel", "parallel")),
        name=name,
    )(a, b, w, w, x, gate)


def _sgate_kernel(u_ref, v_ref, g_ref, ws_ref, bs_ref, o_ref):
    u = _gelu_tanh(u_ref[...])
    v = _gelu_tanh(v_ref[...])
    v = v * lax.rsqrt(jnp.mean(v * v, axis=-1, keepdims=True) + EPS) * g_ref[...]
    vb = v.astype(BF16)
    for c in range(TM // SG_CHUNK):
        rows = slice(c * SG_CHUNK, (c + 1) * SG_CHUNK)
        for g in range(SG_GROUPS):
            cols = slice(g * 128, (g + 1) * 128)
            s = _dot(ws_ref[g], vb[rows, cols]) + bs_ref[g]
            o_ref[rows, cols] = (u[rows, cols] * s).astype(o_ref.dtype)


def _spatial_gating(h, sg_norm, sg_w, sg_b):
    bias = jnp.broadcast_to(sg_b[:, :, None], (SG_GROUPS, SG_CHUNK, 128))
    return pl.pallas_call(
        _sgate_kernel,
        grid=(T_ALL // TM,),
        in_specs=[pl.BlockSpec((TM, SG_DIM), lambda i: (i, 0)),
                  pl.BlockSpec((TM, SG_DIM), lambda i: (i, 1)),
                  pl.BlockSpec((1, SG_DIM), lambda i: (0, 0)),
                  pl.BlockSpec((SG_GROUPS, SG_CHUNK, SG_CHUNK), lambda i: (0, 0, 0)),
                  pl.BlockSpec((SG_GROUPS, SG_CHUNK, 128), lambda i: (0, 0, 0))],
        out_specs=pl.BlockSpec((TM, SG_DIM), lambda i: (i, 0)),
        out_shape=jax.ShapeDtypeStruct((T_ALL, SG_DIM), BF16),
        compiler_params=_cparams(("parallel",)),
        name="spatial_gating",
    )(h, h, sg_norm.reshape(1, SG_DIM), sg_w.astype(BF16), bias)


def _softmax_rows(s):
    e = jnp.exp(s - jnp.max(s, axis=-1, keepdims=True))
    return e / jnp.sum(e, axis=-1, keepdims=True)


def _ctx_attn_kernel(*refs):
    q_ref, k_ref, v_ref = refs[:3]
    o_ref, kv_ref = refs[-2:]
    k = k_ref[...]
    v = v_ref[...]
    s = _dot_nt(q_ref[...].astype(BF16), k.astype(BF16)) * (NA_HEAD_DIM ** -0.5)
    p = _softmax_rows(s).astype(BF16)
    o_ref[...] = _dot(p, v.astype(BF16)).astype(o_ref.dtype)
    kv_ref[0, 0] = k
    kv_ref[1, 0] = v


def _context_attention(h, j, kv_prev):
    qc = 2 * SG_DIM // NA_HEAD_DIM
    in_specs = [pl.BlockSpec((L_CTX, NA_HEAD_DIM), lambda b, hh: (b, qc + hh)),
                pl.BlockSpec((L_CTX, NA_HEAD_DIM), lambda b, hh: (b, qc + NA_HEADS + hh)),
                pl.BlockSpec((L_CTX, NA_HEAD_DIM), lambda b, hh: (b, qc + 2 * NA_HEADS + hh))]
    args = [h, h, h]
    aliases = {}
    if kv_prev is not None:
        in_specs.append(_any_spec())
        args.append(kv_prev)
        aliases = {3: 1}
    return pl.pallas_call(
        _ctx_attn_kernel,
        grid=(N_CTX_SEQ, NA_HEADS),
        in_specs=in_specs,
        out_specs=[pl.BlockSpec((L_CTX, NA_HEAD_DIM), lambda b, hh: (b, hh)),
                   pl.BlockSpec((None, None, 2, 1, L_CTX, NA_HEAD_DIM), lambda b, hh: (b, j, 0, hh, 0, 0))],
        out_shape=[jax.ShapeDtypeStruct((T_ALL, NA_DIM), BF16),
                   jax.ShapeDtypeStruct((N_CTX_SEQ, N_EVEN, 2, NA_HEADS, L_CTX, NA_HEAD_DIM), F32)],
        input_output_aliases=aliases,
        compiler_params=_cparams(("parallel", "parallel")),
        name="context_attention",
    )(*args)


def _na_bias_table(rpb):
    col = np.arange(GRID_W)
    c0 = np.clip(col - WIN_C // 2, 0, GRID_W - WIN_C)
    col_ok = (col[None, :] >= c0[:, None]) & (col[None, :] < c0[:, None] + WIN_C)
    dc = np.clip(col[None, :] - col[:, None], -(WIN_C - 1), WIN_C - 1) + (WIN_C - 1)
    dr = np.arange(WIN_R)[:, None] + np.arange(WIN_R)[None, :]
    b = rpb.astype(F32)[:, dr][..., dc]
    b = jnp.where(col_ok[None, None, None], b, NEG_INF)
    return b.transpose(0, 1, 3, 2, 4).reshape(NA_HEADS, WIN_R, GRID_W, WIN_R * GRID_W)


def _na_attn_kernel(q_ref, k_ref, v_ref, kc_ref, vc_ref, bias_ref, o_in_ref, o_ref, kb, vb, kcb, vcb):
    del o_in_ref
    kb[...] = k_ref[...].astype(BF16)
    vb[...] = v_ref[...].astype(BF16)
    kcb[...] = kc_ref[...].astype(BF16)
    vcb[...] = vc_ref[...].astype(BF16)
    scale = NA_HEAD_DIM ** -0.5
    win = WIN_R * GRID_W

    def row(r, carry):
        r0 = jnp.clip(r - WIN_R // 2, 0, NA_ROWS - WIN_R)
        d0 = r0 - r + (WIN_R - 1)
        qs = pl.multiple_of(r * GRID_W, GRID_W)
        ks = pl.multiple_of(r0 * GRID_W, GRID_W)
        q = q_ref[pl.ds(qs, GRID_W), :].astype(BF16)
        s_loc = _dot_nt(q, kb[pl.ds(ks, win), :]) * scale + bias_ref[d0]
        s_ctx = _dot_nt(q, kcb[...]) * scale
        m = jnp.maximum(jnp.max(s_loc, axis=-1, keepdims=True), jnp.max(s_ctx, axis=-1, keepdims=True))
        e_loc = jnp.exp(s_loc - m)
        e_ctx = jnp.exp(s_ctx - m)
        den = jnp.sum(e_loc, axis=-1, keepdims=True) + jnp.sum(e_ctx, axis=-1, keepdims=True)
        o = _dot((e_loc / den).astype(BF16), vb[pl.ds(ks, win), :]) + _dot((e_ctx / den).astype(BF16), vcb[...])
        o_ref[pl.ds(qs, GRID_W), :] = o.astype(o_ref.dtype)
        return carry

    lax.fori_loop(0, NA_ROWS, row, 0, unroll=NA_UNROLL)


def _neighbourhood_attention(h, cache_kv, j, bias, o_prev):
    qc = 2 * SG_DIM // NA_HEAD_DIM
    rb = T_CTX // L_LAT
    cache_spec = lambda kv: pl.BlockSpec((None, None, None, None, PAST_LEN, NA_HEAD_DIM),
                                         lambda b, hh: (b, j, kv, hh, 0, 0))
    return pl.pallas_call(
        _na_attn_kernel,
        grid=(N_LAT_SEQ, NA_HEADS),
        in_specs=[pl.BlockSpec((L_LAT, NA_HEAD_DIM), lambda b, hh: (rb + b, qc + hh)),
                  pl.BlockSpec((L_LAT, NA_HEAD_DIM), lambda b, hh: (rb + b, qc + NA_HEADS + hh)),
                  pl.BlockSpec((L_LAT, NA_HEAD_DIM), lambda b, hh: (rb + b, qc + 2 * NA_HEADS + hh)),
                  cache_spec(0), cache_spec(1),
                  pl.BlockSpec((None, WIN_R, GRID_W, WIN_R * GRID_W), lambda b, hh: (hh, 0, 0, 0)),
                  _any_spec()],
        out_specs=pl.BlockSpec((L_LAT, NA_HEAD_DIM), lambda b, hh: (rb + b, hh)),
        out_shape=jax.ShapeDtypeStruct((T_ALL, NA_DIM), BF16),
        input_output_aliases={6: 0},
        scratch_shapes=[pltpu.VMEM((L_LAT, NA_HEAD_DIM), BF16), pltpu.VMEM((L_LAT, NA_HEAD_DIM), BF16),
                        pltpu.VMEM((PAST_LEN, NA_HEAD_DIM), BF16), pltpu.VMEM((PAST_LEN, NA_HEAD_DIM), BF16)],
        compiler_params=_cparams(("parallel", "parallel")),
        name="neighbourhood_attention",
    )(h, h, h, cache_kv, cache_kv, bias, o_prev)


CONV_ROWS = 256
CONV_COLS = 512


def _conv_kernel(transpose_out, x_ref, p_ref, n_ref, w_ref, b_ref, o_ref):
    i = pl.program_id(0)
    per = L_LAT // CONV_ROWS
    nct = T_CTX // CONV_ROWS
    pos = jnp.where(i < nct, 0, (i - nct) % per)
    last = jnp.where(i < nct, 0, per - 1)
    x = x_ref[...]
    prev_row = jnp.where(pos == 0, 0.0, p_ref[7:8, :])
    next_row = jnp.where(pos == last, 0.0, n_ref[0:1, :])
    ridx = lax.broadcasted_iota(jnp.int32, x.shape, 0)
    x_dn = jnp.where(ridx == 0, prev_row, pltpu.roll(x, 1, 0))
    x_up = jnp.where(ridx == CONV_ROWS - 1, next_row, pltpu.roll(x, CONV_ROWS - 1, 0))
    y = _silu(x_dn * w_ref[0:1, :] + x * w_ref[1:2, :] + x_up * w_ref[2:3, :] + b_ref[...])
    if transpose_out:
        for c in range(CONV_ROWS // SSD_CHUNK):
            o_ref[c] = y[c * SSD_CHUNK:(c + 1) * SSD_CHUNK, :].T
    else:
        o_ref[...] = y


def _conv_silu(h, conv_w, conv_b, col0, ncols, transpose_out):
    cb0 = (ODD_COL_XBC + col0) // CONV_COLS
    wb0 = col0 // CONV_COLS
    rb = CONV_ROWS // 8
    nblk8 = T_ALL // 8
    cpb = CONV_ROWS // SSD_CHUNK
    w8 = jnp.zeros((8, SSD_CONV_CH), F32).at[:3].set(conv_w)
    if transpose_out:
        out_spec = pl.BlockSpec((cpb, CONV_COLS, SSD_CHUNK), lambda i, j: (i, j, 0))
        out_shape = jax.ShapeDtypeStruct((T_ALL // SSD_CHUNK, ncols, SSD_CHUNK), F32)
    else:
        out_spec = pl.BlockSpec((CONV_ROWS, CONV_COLS), lambda i, j: (i, j))
        out_shape = jax.ShapeDtypeStruct((T_ALL, ncols), F32)
    return pl.pallas_call(
        functools.partial(_conv_kernel, transpose_out),
        grid=(T_ALL // CONV_ROWS, ncols // CONV_COLS),
        in_specs=[pl.BlockSpec((CONV_ROWS, CONV_COLS), lambda i, j: (i, cb0 + j)),
                  pl.BlockSpec((8, CONV_COLS), lambda i, j: (jnp.maximum(i * rb - 1, 0), cb0 + j)),
                  pl.BlockSpec((8, CONV_COLS), lambda i, j: (jnp.minimum((i + 1) * rb, nblk8 - 1), cb0 + j)),
                  pl.BlockSpec((8, CONV_COLS), lambda i, j: (0, wb0 + j)),
                  pl.BlockSpec((1, CONV_COLS), lambda i, j: (0, wb0 + j))],
        out_specs=out_spec,
        out_shape=out_shape,
        compiler_params=_cparams(("parallel", "parallel")),
        name="conv_silu_t" if transpose_out else "conv_silu",
    )(h, h, h, w8, conv_b.reshape(1, SSD_CONV_CH))


def _seq_info(t, chunk):
    nct = L_CTX // chunk
    ncl = L_LAT // chunk
    nctx = N_CTX_SEQ * nct
    is_ctx = t < nctx
    tl = jnp.maximum(t - nctx, 0)
    seq = jnp.where(is_ctx, t // nct, N_CTX_SEQ + tl // ncl)
    c = jnp.where(is_ctx, t % nct, tl % ncl)
    n = jnp.where(is_ctx, nct, ncl)
    first = jnp.where(is_ctx, (t // nct) * nct, nctx + (tl // ncl) * ncl)
    return seq, c, n, first


def _row_blk(t, chunk, rev):
    if not rev:
        return t
    _, c, n, first = _seq_info(t, chunk)
    return first + (n - 1 - c)


def _lat_seq(t, chunk):
    return jnp.maximum(_seq_info(t, chunk)[0] - N_CTX_SEQ, 0)


def _ctx_seq(t, chunk):
    return jnp.minimum(_seq_info(t, chunk)[0], N_CTX_SEQ - 1)


def _head_rows(mat, d, g):
    base = d * SSD_HEADS + g * SSD_GROUP_HEADS
    return jnp.concatenate(
        [jnp.broadcast_to(mat[base + r:base + r + 1, :], (SSD_HEAD_DIM, mat.shape[1])) for r in range(SSD_GROUP_HEADS)],
        axis=0)


def _ssd_kernel(rev, has_prev, *refs):
    refs = list(refs)
    xt_ref, bc_ref, sm_ref, dtb_ref, alog_ref, s0_ref = refs[:6]
    pos = 6
    if rev:
        yf_ref, z_ref, dcol_ref, gn_ref = refs[pos:pos + 4]
        pos += 4
    if has_prev:
        pos += 1
    y_ref, sfin_ref, s_scr = refs[pos:pos + 3]
    yt_scr = refs[pos + 3] if rev else None

    t = pl.program_id(0)
    seq, c, n, _ = _seq_info(t, SSD_CHUNK)
    is_ctx = seq < N_CTX_SEQ
    q = SSD_CHUNK
    d = 1 if rev else 0

    @pl.when(c == 0)
    def _():
        s_scr[...] = jnp.where(is_ctx, 0.0, s0_ref[...])

    ii = lax.broadcasted_iota(jnp.int32, (q, q), 0)
    jj = lax.broadcasted_iota(jnp.int32, (q, q), 1)
    mask = (jj >= ii) if rev else (jj <= ii)
    tri = mask.astype(BF16)

    dt = _softplus(sm_ref[...] + dtb_ref[...])
    dta = dt * (-jnp.exp(alog_ref[...]))
    cum = _mask_dot(tri, dta)
    tot = cum[0:1, :] if rev else cum[q - 1:q, :]
    cum_t = cum.T
    dt_t = dt.T
    te_t = (jnp.exp(tot - cum) * dt).T
    ecum_t = jnp.exp(cum_t)
    last = 0 if rev else q - 1
    etot_t = jnp.broadcast_to(ecum_t[:, last:last + 1], (q, q))

    for g in range(SSD_GROUPS):
        gs = slice(g * SSD_GROUP_DIM, (g + 1) * SSD_GROUP_DIM)
        bg = bc_ref[:, g * SSD_STATE:(g + 1) * SSD_STATE].astype(BF16)
        cg = bc_ref[:, SSD_BC + g * SSD_STATE:SSD_BC + (g + 1) * SSD_STATE].astype(BF16)
        cb = _dot_nt(cg, bg)
        s_g = s_scr[gs, :]
        x_g = xt_ref[gs, :]
        y_off = _dot_nt(s_g.astype(BF16), cg) * _head_rows(ecum_t, d, g)
        st = _dot((x_g * _head_rows(te_t, d, g)).astype(BF16), bg)
        s_scr[gs, :] = s_g * _head_rows(etot_t, d, g) + st
        for k in range(SSD_GROUP_HEADS // 2):
            w_pair = []
            for r in (2 * k, 2 * k + 1):
                col = d * SSD_HEADS + g * SSD_GROUP_HEADS + r
                seg = cum[:, col:col + 1] - cum_t[col:col + 1, :]
                w_pair.append((cb * jnp.where(mask, jnp.exp(seg), 0.0) * dt_t[col:col + 1, :]).astype(BF16))
            w2 = jnp.concatenate(w_pair, axis=1)
            ps = slice(k * 2 * SSD_HEAD_DIM, (k + 1) * 2 * SSD_HEAD_DIM)
            xp = x_g[ps, :].astype(BF16)
            rr = lax.broadcasted_iota(jnp.int32, xp.shape, 0)
            zero = jnp.zeros_like(xp)
            x2 = jnp.concatenate([jnp.where(rr < SSD_HEAD_DIM, xp, zero), jnp.where(rr >= SSD_HEAD_DIM, xp, zero)],
                                 axis=1)
            y_pair = _dot_nt(x2, w2) + y_off[ps, :]
            rows = slice(g * SSD_GROUP_DIM + k * 2 * SSD_HEAD_DIM, g * SSD_GROUP_DIM + (k + 1) * 2 * SSD_HEAD_DIM)
            if rev:
                yt_scr[rows, :] = y_pair
            else:
                y_ref[rows, :] = y_pair

    if rev:
        yt = yf_ref[...] + yt_scr[...] + dcol_ref[...] * xt_ref[...]
        y = yt.T * _silu(z_ref[...])
        y = y * lax.rsqrt(jnp.mean(y * y, axis=-1, keepdims=True) + EPS) * gn_ref[...]
        y_ref[...] = y.astype(y_ref.dtype)

    @pl.when((c == n - 1) & is_ctx)
    def _():
        sfin_ref[...] = s_scr[...]


def _ssd_scan(rev, layer, xt, bc, small, dtb, alog, s0, states_prev, extra=None):
    q = SSD_CHUNK
    d = 1 if rev else 0
    blk = lambda t: _row_blk(t, q, rev)
    in_specs = [pl.BlockSpec((None, SSD_DIM, q), lambda t: (blk(t), 0, 0)),
                pl.BlockSpec((q, 2 * SSD_BC), lambda t: (blk(t), 0)),
                pl.BlockSpec((q, SMALL_W), lambda t: (blk(t), 0)),
                pl.BlockSpec((1, SMALL_W), lambda t: (0, 0)),
                pl.BlockSpec((1, SMALL_W), lambda t: (0, 0)),
                pl.BlockSpec((None, SSD_DIM, SSD_STATE), lambda t: (_lat_seq(t, q), 0, 0))]
    args = [xt, bc, small, dtb, alog, s0]
    if rev:
        yf, h, dcol, gn = extra
        in_specs += [pl.BlockSpec((None, SSD_DIM, q), lambda t: (blk(t), 0, 0)),
                     pl.BlockSpec((q, SSD_DIM), lambda t: (blk(t), 0)),
                     pl.BlockSpec((SSD_DIM, q), lambda t: (0, 0)),
                     pl.BlockSpec((1, SSD_DIM), lambda t: (0, 0))]
        args += [yf, h, dcol, gn]
        y_spec = pl.BlockSpec((q, SSD_DIM), lambda t: (blk(t), 0))
        y_shape = jax.ShapeDtypeStruct((T_ALL, SSD_DIM), BF16)
    else:
        y_spec = pl.BlockSpec((None, SSD_DIM, q), lambda t: (blk(t), 0, 0))
        y_shape = jax.ShapeDtypeStruct((T_ALL // q, SSD_DIM, q), F32)
    aliases = {}
    if states_prev is not None:
        aliases = {len(args): 1}
        in_specs.append(_any_spec())
        args.append(states_prev)
    scratch = [pltpu.VMEM((SSD_DIM, SSD_STATE), F32)]
    if rev:
        scratch.append(pltpu.VMEM((SSD_DIM, q), F32))
    return pl.pallas_call(
        functools.partial(_ssd_kernel, rev, states_prev is not None),
        grid=(T_ALL // q,),
        in_specs=in_specs,
        out_specs=[y_spec,
                   pl.BlockSpec((None, None, None, SSD_DIM, SSD_STATE), lambda t: (_ctx_seq(t, q), layer, d, 0, 0))],
        out_shape=[y_shape, jax.ShapeDtypeStruct((N_CTX_SEQ, N_ODD, 2, SSD_DIM, SSD_STATE), F32)],
        input_output_aliases=aliases,
        scratch_shapes=scratch,
        compiler_params=_cparams(("arbitrary",)),
        name="ssd_bwd" if rev else "ssd_fwd",
    )(*args)


def _gla_kernel(rev, has_prev, *refs):
    refs = list(refs)
    q_ref, k_ref, v_ref, sm_ref, wa_ref, ba_ref, s0_ref = refs[:7]
    pos = 7
    if rev:
        of_ref, gate_ref, gn_ref = refs[pos:pos + 3]
        pos += 3
    if has_prev:
        pos += 1
    o_ref, sfin_ref, s_scr = refs[pos:pos + 3]

    t = pl.program_id(0)
    seq, c, n, _ = _seq_info(t, GLA_CHUNK)
    is_ctx = seq < N_CTX_SEQ
    cq = GLA_CHUNK

    @pl.when(c == 0)
    def _():
        for hh in range(GLA_HEADS):
            s_scr[hh] = jnp.where(is_ctx, 0.0, s0_ref[hh].T)

    ii = lax.broadcasted_iota(jnp.int32, (cq, cq), 0)
    jj = lax.broadcasted_iota(jnp.int32, (cq, cq), 1)
    mask = (jj >= ii) if rev else (jj <= ii)
    tri = mask.astype(BF16)

    g_lin = _dot(sm_ref[...].astype(BF16), wa_ref[...]) + ba_ref[...]
    g = -_softplus(-g_lin) / GLA_GATE_NORM
    cum = _mask_dot(tri, g)
    last = cum[0:1, :] if rev else cum[cq - 1:cq, :]
    q_e = (q_ref[...] * (GLA_HEAD_K ** -0.5)) * jnp.exp(cum)
    k = k_ref[...]
    k_e = k * jnp.exp(-cum)
    k_end = k * jnp.exp(last - cum)
    e_last = jnp.exp(last)

    for hh in range(GLA_HEADS):
        ks = slice(hh * GLA_HEAD_K, (hh + 1) * GLA_HEAD_K)
        vs = slice(hh * GLA_HEAD_V, (hh + 1) * GLA_HEAD_V)
        qh = q_e[:, ks].astype(BF16)
        vh = v_ref[:, vs]
        vh_b = vh.astype(BF16)
        s_prev = s_scr[hh]
        att = jnp.where(mask, _dot_nt(qh, k_e[:, ks].astype(BF16)), 0.0)
        o = _dot(att.astype(BF16), vh_b) + _dot_nt(qh, s_prev.astype(BF16))
        st = _dot(vh.T.astype(BF16), k_end[:, ks].astype(BF16))
        s_scr[hh] = s_prev * e_last[:, ks] + st
        if rev:
            o = o + of_ref[:, vs]
            o = o * lax.rsqrt(jnp.mean(o * o, axis=-1, keepdims=True) + EPS) * gn_ref[...]
            o_ref[:, vs] = (o * _silu(gate_ref[:, vs])).astype(o_ref.dtype)
        else:
            o_ref[:, vs] = o

    @pl.when((c == n - 1) & is_ctx)
    def _():
        for hh in range(GLA_HEADS):
            sfin_ref[hh] = s_scr[hh].T


def _gla_scan(rev, layer, h, small, wa, ba, s0, states_prev, extra=None):
    cq = GLA_CHUNK
    d = 1 if rev else 0
    cv = SSD_DIM // GLA_V_DIM
    cqk = (SSD_DIM + 2 * GLA_V_DIM) // GLA_QK_DIM
    row = lambda t: _row_blk(t, cq, rev)
    in_specs = [pl.BlockSpec((cq, GLA_QK_DIM), lambda t: (row(t), cqk)),
                pl.BlockSpec((cq, GLA_QK_DIM), lambda t: (row(t), cqk + 1)),
                pl.BlockSpec((cq, GLA_V_DIM), lambda t: (row(t), cv)),
                pl.BlockSpec((cq, SMALL_W), lambda t: (row(t), 0)),
                pl.BlockSpec((SMALL_W, GLA_QK_DIM), lambda t: (0, 0)),
                pl.BlockSpec((1, GLA_QK_DIM), lambda t: (0, 0)),
                pl.BlockSpec((None, GLA_HEADS, GLA_HEAD_K, GLA_HEAD_V), lambda t: (_lat_seq(t, cq), 0, 0, 0))]
    args = [h, h, h, small, wa, ba, s0]
    if rev:
        o_f, gn = extra
        in_specs += [pl.BlockSpec((cq, GLA_V_DIM), lambda t: (row(t), 0)),
                     pl.BlockSpec((cq, GLA_V_DIM), lambda t: (row(t), cv + 1)),
                     pl.BlockSpec((1, GLA_HEAD_V), lambda t: (0, 0))]
        args += [o_f, h, gn]
    aliases = {}
    if states_prev is not None:
        aliases = {len(args): 1}
        in_specs.append(_any_spec())
        args.append(states_prev)
    return pl.pallas_call(
        functools.partial(_gla_kernel, rev, states_prev is not None),
        grid=(T_ALL // cq,),
        in_specs=in_specs,
        out_specs=[pl.BlockSpec((cq, GLA_V_DIM), lambda t: (row(t), 0)),
                   pl.BlockSpec((None, None, None, GLA_HEADS, GLA_HEAD_K, GLA_HEAD_V),
                                lambda t: (_ctx_seq(t, cq), layer, d, 0, 0, 0))],
        out_shape=[jax.ShapeDtypeStruct((T_ALL, GLA_V_DIM), BF16 if rev else F32),
                   jax.ShapeDtypeStruct((N_CTX_SEQ, N_ODD, 2, GLA_HEADS, GLA_HEAD_K, GLA_HEAD_V), F32)],
        input_output_aliases=aliases,
        scratch_shapes=[pltpu.VMEM((GLA_HEADS, GLA_HEAD_V, GLA_HEAD_K), F32)],
        compiler_params=_cparams(("arbitrary",)),
        name="gla_bwd" if rev else "gla_fwd",
    )(*args)


ROUTER_ROWS = 32


def _first_max(vals):
    out = []
    for a, va in enumerate(vals):
        ok = None
        for b, vb in enumerate(vals):
            if a == b:
                continue
            t = (va > vb) if b < a else (va >= vb)
            ok = t if ok is None else (ok & t)
        out.append(ok)
    return out


def _router_kernel(x_ref, g_ref, sc_ref, sh_ref, whi_ref, wlo_ref, bsel_ref, tri_ref,
                   hn_ref, gates_ref, rank_ref, cnt_ref, carry):
    i = pl.program_id(0)

    @pl.when(i == 0)
    def _():
        carry[...] = jnp.zeros_like(carry)

    hn = _norm_mod(x_ref[...], g_ref[...], sc_ref[...], sh_ref[...])
    hn_ref[...] = hn
    hb = hn.astype(BF16)
    hl = (hn - hb.astype(F32)).astype(BF16)
    logits = _dot(hb, whi_ref[...]) + _dot(hb, wlo_ref[...]) + _dot(hl, whi_ref[...])
    lt = logits.T
    score = [jax.nn.sigmoid(lt[8 * m:8 * m + 8, :]) for m in range(EXPERTS_PER_GROUP)]
    sel = [score[m] + bsel_ref[8 * m:8 * m + 8, :] for m in range(EXPERTS_PER_GROUP)]

    hi1, lo1 = jnp.maximum(sel[0], sel[1]), jnp.minimum(sel[0], sel[1])
    hi2, lo2 = jnp.maximum(sel[2], sel[3]), jnp.minimum(sel[2], sel[3])
    grp = jnp.maximum(hi1, hi2) + jnp.maximum(jnp.minimum(hi1, hi2), jnp.maximum(lo1, lo2))
    best = _first_max([grp[g:g + 1, :] for g in range(N_EXPERT_GROUPS)])
    srow = lax.broadcasted_iota(jnp.int32, grp.shape, 0)
    gmask = jnp.zeros(grp.shape, jnp.bool_)
    for g in range(N_EXPERT_GROUPS):
        gmask = gmask | ((srow == g) & best[g])

    first = _first_max(sel)
    sel2 = [jnp.where(first[m], -jnp.inf, sel[m]) for m in range(EXPERTS_PER_GROUP)]
    second = _first_max(sel2)
    chosen = [(first[m] | second[m]) & gmask for m in range(EXPERTS_PER_GROUP)]
    wsum = sum(jnp.where(chosen[m], score[m], 0.0) for m in range(EXPERTS_PER_GROUP))
    wtot = jnp.sum(wsum, axis=0, keepdims=True)
    cm = jnp.concatenate([chosen[m].astype(F32) for m in range(EXPERTS_PER_GROUP)], axis=0)
    sc_all = jnp.concatenate(score, axis=0)
    gates_ref[...] = jnp.where(cm > 0, sc_all / wtot, 0.0)
    within = _dot(cm.astype(BF16), tri_ref[...])
    rank_ref[...] = jnp.where(cm > 0, carry[:, 0:1] + within, -1.0)
    carry[...] = carry[...] + jnp.sum(cm, axis=1, keepdims=True)
    cnt_ref[...] = carry[...]


def _router(x, g, sc, sh, w_router, b_router):
    slot_of_expert = np.array([(e % EXPERTS_PER_GROUP) * 8 + e // EXPERTS_PER_GROUP for e in range(N_EXPERTS)])
    w128 = jnp.zeros((D_MODEL, 128), F32).at[:, slot_of_expert].set(w_router)
    w_hi = w128.astype(BF16)
    w_lo = (w128 - w_hi.astype(F32)).astype(BF16)
    b32 = jnp.zeros((ROUTER_ROWS,), F32).at[slot_of_expert].set(b_router.astype(F32))
    bsel = jnp.broadcast_to(b32[:, None], (ROUTER_ROWS, TM))
    tri = jnp.asarray(np.triu(np.ones((TM, TM), np.float32), 1), BF16)
    mod_spec = pl.BlockSpec((None, 1, D_MODEL), lambda i: (_mod_row(i, TM), 0, 0))
    hn, gates, rank, cnt = pl.pallas_call(
        _router_kernel,
        grid=(T_ALL // TM,),
        in_specs=[pl.BlockSpec((TM, D_MODEL), lambda i: (i, 0)),
                  pl.BlockSpec((1, D_MODEL), lambda i: (0, 0)),
                  mod_spec, mod_spec,
                  pl.BlockSpec((D_MODEL, 128), lambda i: (0, 0)),
                  pl.BlockSpec((D_MODEL, 128), lambda i: (0, 0)),
                  pl.BlockSpec((ROUTER_ROWS, TM), lambda i: (0, 0)),
                  pl.BlockSpec((TM, TM), lambda i: (0, 0))],
        out_specs=[pl.BlockSpec((TM, D_MODEL), lambda i: (i, 0)),
                   pl.BlockSpec((ROUTER_ROWS, TM), lambda i: (0, i)),
                   pl.BlockSpec((ROUTER_ROWS, TM), lambda i: (0, i)),
                   pl.BlockSpec((ROUTER_ROWS, 128), lambda i: (0, 0))],
        out_shape=[jax.ShapeDtypeStruct((T_ALL, D_MODEL), F32),
                   jax.ShapeDtypeStruct((ROUTER_ROWS, T_ALL), F32),
                   jax.ShapeDtypeStruct((ROUTER_ROWS, T_ALL), F32),
                   jax.ShapeDtypeStruct((ROUTER_ROWS, 128), F32)],
        scratch_shapes=[pltpu.VMEM((ROUTER_ROWS, 128), F32)],
        compiler_params=_cparams(("arbitrary",)),
        name="router",
    )(x, g.reshape(1, D_MODEL), sc, sh, w_hi, w_lo, bsel, tri)
    return hn, gates[slot_of_expert], rank[slot_of_expert], cnt[slot_of_expert, 0]


def _dispatch_plan(gates, rank, cnt):
    cnt = cnt.astype(jnp.int32)
    padded = ((cnt + MOE_TM - 1) // MOE_TM) * MOE_TM
    end = jnp.cumsum(padded)
    start = end - padded
    chosen = rank >= 0
    slot = start[:, None] + rank.astype(jnp.int32)
    slot_a = jnp.min(jnp.where(chosen, slot, MOE_ROWS), axis=0)
    slot_b = jnp.max(jnp.where(chosen, slot, -1), axis=0)
    w_a = jnp.sum(jnp.where(chosen & (slot == slot_a[None]), gates, 0.0), axis=0)
    w_b = jnp.sum(jnp.where(chosen & (slot == slot_b[None]), gates, 0.0), axis=0)
    tile_row = jnp.arange(MOE_TILES, dtype=jnp.int32) * MOE_TM
    tile_expert = jnp.minimum(jnp.sum(tile_row[:, None] >= end[None, :], axis=1), N_EXPERTS - 1).astype(jnp.int32)
    changed = jnp.concatenate([jnp.ones((1,), jnp.bool_), tile_expert[1:] != tile_expert[:-1]])
    tile_state = jnp.where(tile_row < end[-1], 1 + changed.astype(jnp.int32), 0).astype(jnp.int32)
    return slot_a, slot_b, w_a, w_b, tile_expert, tile_state


def _dispatch_kernel(sa_ref, sb_ref, hn_ref, xs_in_ref, xs_ref, sem):
    del xs_in_ref
    base = pl.program_id(0) * TM

    def row_copies(r):
        src = hn_ref.at[pl.ds(r, 1)]
        return (pltpu.make_async_copy(src, xs_ref.at[pl.ds(sa_ref[base + r], 1)], sem),
                pltpu.make_async_copy(src, xs_ref.at[pl.ds(sb_ref[base + r], 1)], sem))

    def start(r, carry):
        for cp in row_copies(r):
            cp.start()
        return carry

    def wait(r, carry):
        for cp in row_copies(r):
            cp.wait()
        return carry

    lax.fori_loop(0, TM, start, 0)
    lax.fori_loop(0, TM, wait, 0)


def _dispatch(hn, slot_a, slot_b):
    xs0 = jnp.zeros((MOE_ROWS, D_MODEL), F32)
    return pl.pallas_call(
        _dispatch_kernel,
        grid_spec=pltpu.PrefetchScalarGridSpec(
            num_scalar_prefetch=2,
            grid=(T_ALL // TM,),
            in_specs=[pl.BlockSpec((TM, D_MODEL), lambda i, sa, sb: (i, 0)), _any_spec()],
            out_specs=_any_spec(),
            scratch_shapes=[pltpu.SemaphoreType.DMA(())]),
        out_shape=jax.ShapeDtypeStruct((MOE_ROWS, D_MODEL), F32),
        input_output_aliases={3: 0},
        compiler_params=_cparams(("arbitrary",)),
        name="moe_dispatch",
    )(slot_a, slot_b, hn, xs0)


def _moe_up_kernel(te_ref, ts_ref, x_ref, wg_ref, wu_ref, act_ref, wg_b, wu_b):
    state = ts_ref[pl.program_id(0)]

    @pl.when(state == 2)
    def _():
        wg_b[...] = wg_ref[...].astype(BF16)
        wu_b[...] = wu_ref[...].astype(BF16)

    @pl.when(state > 0)
    def _():
        x = x_ref[...].astype(BF16)
        act_ref[...] = (_silu(_dot(x, wg_b[...])) * _dot(x, wu_b[...])).astype(BF16)

    @pl.when(state == 0)
    def _():
        act_ref[...] = jnp.zeros_like(act_ref)


def _moe_down_kernel(te_ref, ts_ref, act_ref, wd_ref, o_ref, wd_b):
    state = ts_ref[pl.program_id(0)]

    @pl.when(state == 2)
    def _():
        wd_b[...] = wd_ref[...].astype(BF16)

    @pl.when(state > 0)
    def _():
        o_ref[...] = _dot(act_ref[...], wd_b[...])

    @pl.when(state == 0)
    def _():
        o_ref[...] = jnp.zeros_like(o_ref)


def _moe_experts(layer, tile_expert, tile_state, xs, w_gate, w_up, w_down):
    w_up_spec = pl.BlockSpec((None, None, D_MODEL, D_FF), lambda i, te, ts: (layer, te[i], 0, 0))
    act = pl.pallas_call(
        _moe_up_kernel,
        grid_spec=pltpu.PrefetchScalarGridSpec(
            num_scalar_prefetch=2,
            grid=(MOE_TILES,),
            in_specs=[pl.BlockSpec((MOE_TM, D_MODEL), lambda i, te, ts: (i, 0)), w_up_spec, w_up_spec],
            out_specs=pl.BlockSpec((MOE_TM, D_FF), lambda i, te, ts: (i, 0)),
            scratch_shapes=[pltpu.VMEM((D_MODEL, D_FF), BF16), pltpu.VMEM((D_MODEL, D_FF), BF16)]),
        out_shape=jax.ShapeDtypeStruct((MOE_ROWS, D_FF), BF16),
        compiler_params=_cparams(("arbitrary",)),
        name="moe_gate_up",
    )(tile_expert, tile_state, xs, w_gate, w_up)
    return pl.pallas_call(
        _moe_down_kernel,
        grid_spec=pltpu.PrefetchScalarGridSpec(
            num_scalar_prefetch=2,
            grid=(MOE_TILES,),
            in_specs=[pl.BlockSpec((MOE_TM, D_FF), lambda i, te, ts: (i, 0)),
                      pl.BlockSpec((None, None, D_FF, D_MODEL), lambda i, te, ts: (layer, te[i], 0, 0))],
            out_specs=pl.BlockSpec((MOE_TM, D_MODEL), lambda i, te, ts: (i, 0)),
            scratch_shapes=[pltpu.VMEM((D_FF, D_MODEL), BF16)]),
        out_shape=jax.ShapeDtypeStruct((MOE_ROWS, D_MODEL), F32),
        compiler_params=_cparams(("arbitrary",)),
        name="moe_down",
    )(tile_expert, tile_state, act, w_down)


def _combine_kernel(x_ref, g_ref, ya_ref, yb_ref, wa_ref, wb_ref, o_ref):
    o_ref[...] = x_ref[...] + g_ref[...] * (wa_ref[...] * ya_ref[...] + wb_ref[...] * yb_ref[...])


def _combine(x, gate, ya, yb, w_a, w_b):
    row = pl.BlockSpec((TM, D_MODEL), lambda i: (i, 0))
    wspec = pl.BlockSpec((TM, 1), lambda i: (i, 0))
    return pl.pallas_call(
        _combine_kernel,
        grid=(T_ALL // TM,),
        in_specs=[row, pl.BlockSpec((None, 1, D_MODEL), lambda i: (_mod_row(i, TM), 0, 0)), row, row, wspec, wspec],
        out_specs=row,
        out_shape=jax.ShapeDtypeStruct((T_ALL, D_MODEL), F32),
        compiler_params=_cparams(("parallel",)),
        name="moe_combine",
    )(x, gate, ya, yb, w_a.reshape(T_ALL, 1), w_b.reshape(T_ALL, 1))


def _take_rows(a, idx):
    return a.at[idx].get(mode="promise_in_bounds")


def _moe_layer(layer, x, g, sc, sh, gate, w_router, b_router, w_gate, w_up, w_down):
    hn, gates, rank, cnt = _router(x, g, sc, sh, w_router, b_router)
    slot_a, slot_b, w_a, w_b, tile_expert, tile_state = _dispatch_plan(gates, rank, cnt)
    xs = _dispatch(hn, slot_a, slot_b)
    y = _moe_experts(layer, tile_expert, tile_state, xs, w_gate, w_up, w_down)
    return _combine(x, gate, _take_rows(y, slot_a), _take_rows(y, slot_b), w_a, w_b)


def _final_norm_kernel(x_ref, g_ref, o_ref):
    x = x_ref[...]
    o_ref[...] = x * lax.rsqrt(jnp.mean(x * x, axis=-1, keepdims=True) + EPS) * g_ref[...]


def _final_norm(x, g):
    row = pl.BlockSpec((TM, D_MODEL), lambda i: (i, 0))
    return pl.pallas_call(
        _final_norm_kernel,
        grid=(T_ALL // TM,),
        in_specs=[row, pl.BlockSpec((1, D_MODEL), lambda i: (0, 0))],
        out_specs=row,
        out_shape=jax.ShapeDtypeStruct((T_ALL, D_MODEL), F32),
        compiler_params=_cparams(("parallel",)),
        name="final_norm",
    )(x, g.reshape(1, D_MODEL))


def _even_layer(x, mods, j, cache_kv, kv_new, norm_g, w_in, w_out, sg_norm, sg_w, sg_b, rpb):
    sh1, sc1, g1 = mods
    h = _inproj(x, norm_g, sc1, sh1, w_in, "inproj_even")
    a_out = _spatial_gating(h, sg_norm, sg_w, sg_b)
    o, kv_new = _context_attention(h, j, kv_new)
    o = _neighbourhood_attention(h, cache_kv, j, _na_bias_table(rpb), o)
    return _outproj(a_out, o, w_out, x, g1, "outproj_even"), kv_new


def _odd_layer(x, mods, j, state_ssm, state_gla, ssm_new, gla_new, norm_g, w_in, w_out, conv_w, conv_b, dt_bias,
               a_log, ssd_d, ssd_norm, w_a2, b_a, gla_norm):
    sh1, sc1, g1 = mods
    o = np.cumsum([0, SSD_DIM, SSD_CONV_CH, 2 * SSD_HEADS, GLA_QK_DIM, GLA_QK_DIM, GLA_V_DIM, GLA_V_DIM, 2 * GLA_RANK])
    seg = lambda s: w_in[:, o[s]:o[s + 1]]
    w_main = jnp.concatenate([seg(0), seg(5), seg(6), seg(3), seg(4), seg(1)], axis=1)
    w_small = jnp.concatenate([seg(2), seg(7), jnp.zeros((D_MODEL, SMALL_W - 2 * SSD_HEADS - 2 * GLA_RANK), F32)],
                              axis=1)
    h = _inproj(x, norm_g, sc1, sh1, w_main, "inproj_odd")
    small = _inproj(x, norm_g, sc1, sh1, w_small, "inproj_odd_small")

    xt = _conv_silu(h, conv_w, conv_b, 0, SSD_DIM, True)
    bc = _conv_silu(h, conv_w, conv_b, SSD_DIM, 2 * SSD_BC, False)
    pad = jnp.zeros((SMALL_W - 2 * SSD_HEADS,), F32)
    dtb = jnp.concatenate([dt_bias.reshape(-1).astype(F32), pad]).reshape(1, SMALL_W)
    alog = jnp.concatenate([a_log.reshape(-1).astype(F32), pad]).reshape(1, SMALL_W)
    s0 = state_ssm.reshape(N_LAT_SEQ, 2, SSD_DIM, SSD_STATE)
    y_f, ssm_new = _ssd_scan(False, j, xt, bc, small, dtb, alog, s0[:, 0], ssm_new)
    dcol = jnp.broadcast_to(jnp.repeat(ssd_d.astype(F32), SSD_HEAD_DIM)[:, None], (SSD_DIM, SSD_CHUNK))
    y_c, ssm_new = _ssd_scan(True, j, xt, bc, small, dtb, alog, s0[:, 1], ssm_new,
                             (y_f, h, dcol, ssd_norm.reshape(1, SSD_DIM)))

    def wa_pad(dd):
        lo = 2 * SSD_HEADS + dd * GLA_RANK
        return jnp.zeros((SMALL_W, GLA_QK_DIM), F32).at[lo:lo + GLA_RANK].set(w_a2[dd]).astype(BF16)

    o_f, gla_new = _gla_scan(False, j, h, small, wa_pad(0), b_a[0].reshape(1, GLA_QK_DIM), state_gla[:, 0], gla_new)
    y_d, gla_new = _gla_scan(True, j, h, small, wa_pad(1), b_a[1].reshape(1, GLA_QK_DIM), state_gla[:, 1], gla_new,
                             (o_f, gla_norm.reshape(1, GLA_HEAD_V)))

    return _outproj(y_c, y_d, w_out, x, g1, "outproj_odd"), ssm_new, gla_new


def kernel(x_prompt, x_sample, cache_kv, state_ssm, state_gla, c, c_ctx, w_mod, b_mod, norm_mix, norm_ffn,
           norm_final, w_in_even, w_out_even, sg_norm, sg_w, sg_b, na_rpb, w_in_odd, w_out_odd, ssd_conv_w,
           ssd_conv_b, ssd_dt_bias, ssd_a_log, ssd_d, ssd_norm, gla_w_a2, gla_b_a, gla_norm, w_router, b_router,
           w_gate, w_up, w_down):
    x = jnp.concatenate([x_prompt.reshape(T_CTX, D_MODEL), x_sample.reshape(T_LAT, D_MODEL)], axis=0)
    c_rows = jnp.concatenate([c_ctx[None, :], c, jnp.zeros((N_MOD_ROWS - 1 - N_LAT_SEQ, D_MODEL), F32)], axis=0)
    mod = _modulation(c_rows, w_mod, b_mod)

    kv_new, ssm_new, gla_new = None, None, None
    for i in range(DEPTH):
        m = mod[i, :1 + N_LAT_SEQ].reshape(1 + N_LAT_SEQ, N_MOD, 1, D_MODEL)
        sh1, sc1, g1, sh2, sc2, g2 = [m[:, s] for s in range(N_MOD)]
        j = i // 2
        if i % 2 == 0:
            x, kv_new = _even_layer(x, (sh1, sc1, g1), j, cache_kv, kv_new, norm_mix[i], w_in_even[j],
                                    w_out_even[j], sg_norm[j], sg_w[j], sg_b[j], na_rpb[j])
        else:
            x, ssm_new, gla_new = _odd_layer(x, (sh1, sc1, g1), j, state_ssm[:, j], state_gla[:, j], ssm_new,
                                             gla_new, norm_mix[i], w_in_odd[j], w_out_odd[j], ssd_conv_w[j],
                                             ssd_conv_b[j], ssd_dt_bias[j], ssd_a_log[j], ssd_d[j], ssd_norm[j],
                                             gla_w_a2[j], gla_b_a[j], gla_norm[j])
        x = _moe_layer(i, x, norm_ffn[i], sc2, sh2, g2, w_router, b_router, w_gate, w_up, w_down)

    y = _final_norm(x, norm_final)
    return (y[:T_CTX].reshape(N_CTX_SEQ, L_CTX, D_MODEL),
            y[T_CTX:].reshape(N_LAT_SEQ, L_LAT, D_MODEL),
            kv_new,
            ssm_new.reshape(N_CTX_SEQ, N_ODD, 2, SSD_HEADS, SSD_HEAD_DIM, SSD_STATE),
            gla_new)
```

```python
import functools

import numpy as np
import jax
import jax.numpy as jnp
from jax import lax
from jax.experimental import pallas as pl
from jax.experimental.pallas import tpu as pltpu

F32 = jnp.float32
BF16 = jnp.bfloat16

D_MODEL = 2048
DEPTH = 4
N_EVEN = 2
N_ODD = 2
N_MOD = 6
EPS = 1e-6
NEG_INF = -1e30

N_CTX_SEQ = 16
L_CTX = 256
N_LAT_SEQ = 2
L_LAT = 2048
PAST_LEN = 512
T_CTX = N_CTX_SEQ * L_CTX
T_LAT = N_LAT_SEQ * L_LAT
T_ALL = T_CTX + T_LAT
N_MOD_ROWS = 8

GRID_W = 64
SG_DIM = 1024
SG_GROUPS = 8
SG_CHUNK = 128
NA_DIM = 1024
NA_HEADS = 8
NA_HEAD_DIM = 128
WIN_R = 8
WIN_C = 16
NA_ROWS = L_LAT // GRID_W
NA_UNROLL = 4

SSD_DIM = 2048
SSD_HEADS = 32
SSD_HEAD_DIM = 64
SSD_GROUPS = 4
SSD_GROUP_HEADS = SSD_HEADS // SSD_GROUPS
SSD_GROUP_DIM = SSD_GROUP_HEADS * SSD_HEAD_DIM
SSD_STATE = 128
SSD_CHUNK = 128
SSD_BC = SSD_GROUPS * SSD_STATE
SSD_CONV_CH = SSD_DIM + 2 * SSD_BC
GLA_HEADS = 4
GLA_QK_DIM = 1024
GLA_V_DIM = 2048
GLA_HEAD_K = 256
GLA_HEAD_V = 512
GLA_RANK = 16
GLA_GATE_NORM = 16.0
GLA_CHUNK = 64
SMALL_W = 128
ODD_COL_XBC = SSD_DIM + 2 * GLA_V_DIM + 2 * GLA_QK_DIM

N_EXPERTS = 16
N_EXPERT_GROUPS = 4
EXPERTS_PER_GROUP = 4
D_FF = 1024
MOE_TM = 256
MOE_TILES = (2 * T_ALL) // MOE_TM + N_EXPERTS
MOE_ROWS = MOE_TILES * MOE_TM

DISPATCH_UNROLL = 8
TM = 512
PROJ_TM = 1024
VMEM_LIMIT = 56 * 1024 * 1024


def _cparams(sem):
    return pltpu.CompilerParams(dimension_semantics=sem, vmem_limit_bytes=VMEM_LIMIT)


def _mod_row(i, tm):
    nct = T_CTX // tm
    per = L_LAT // tm
    return jnp.where(i < nct, 0, 1 + (i - nct) // per)


def _silu(x):
    return x * jax.nn.sigmoid(x)


def _softplus(x):
    return jnp.maximum(x, 0.0) + jnp.log1p(jnp.exp(-jnp.abs(x)))


def _gelu_tanh(x):
    return 0.5 * x * (1.0 + jnp.tanh(np.sqrt(2.0 / np.pi).astype(np.float32) * (x + 0.044715 * (x * x * x))))


def _split3(x):
    hi = x.astype(BF16)
    r1 = x - hi.astype(F32)
    mid = r1.astype(BF16)
    lo = (r1 - mid.astype(F32)).astype(BF16)
    return hi, mid, lo


def _dot(a, b):
    return jnp.dot(a, b, preferred_element_type=F32)


def _dot_nt(a, b):
    return lax.dot_general(a, b, (((1,), (1,)), ((), ())), preferred_element_type=F32)


def _mask_dot(mask_bf16, x):
    hi, mid, lo = _split3(x)
    return _dot(mask_bf16, hi) + _dot(mask_bf16, mid) + _dot(mask_bf16, lo)


def _any_spec():
    return pl.BlockSpec(memory_space=pl.ANY)


def _mod_kernel(c_ref, w_ref, b_ref, o_ref):
    s = _silu(c_ref[...]).astype(BF16)
    o_ref[...] = _dot(s, w_ref[...].astype(BF16)) + b_ref[...]


def _modulation(c_rows, w_mod, b_mod):
    tn = 1024
    n = N_MOD * D_MODEL
    return pl.pallas_call(
        _mod_kernel,
        grid=(DEPTH, n // tn),
        in_specs=[pl.BlockSpec((N_MOD_ROWS, D_MODEL), lambda l, j: (0, 0)),
                  pl.BlockSpec((None, D_MODEL, tn), lambda l, j: (l, 0, j)),
                  pl.BlockSpec((None, 1, tn), lambda l, j: (l, 0, j))],
        out_specs=pl.BlockSpec((None, N_MOD_ROWS, tn), lambda l, j: (l, 0, j)),
        out_shape=jax.ShapeDtypeStruct((DEPTH, N_MOD_ROWS, n), F32),
        compiler_params=_cparams(("parallel", "parallel")),
        name="modulation",
    )(c_rows, w_mod, b_mod.reshape(DEPTH, 1, n))


def _norm_mod(x, g, sc, sh):
    y = x * lax.rsqrt(jnp.mean(x * x, axis=-1, keepdims=True) + EPS) * g
    return y * (1.0 + sc) + sh


def _inproj_kernel(x_ref, g_ref, sc_ref, sh_ref, w_ref, o_ref, hn_ref):
    @pl.when(pl.program_id(1) == 0)
    def _():
        hn_ref[...] = _norm_mod(x_ref[...], g_ref[...], sc_ref[...], sh_ref[...]).astype(BF16)

    o_ref[...] = _dot(hn_ref[...], w_ref[...].astype(BF16))


def _inproj(x, g, sc, sh, w, name):
    n = w.shape[1]
    tm = PROJ_TM
    tn = min(n, 1024)
    mod_spec = pl.BlockSpec((None, 1, D_MODEL), lambda i, j: (_mod_row(i, tm), 0, 0))
    return pl.pallas_call(
        _inproj_kernel,
        grid=(T_ALL // tm, n // tn),
        in_specs=[pl.BlockSpec((tm, D_MODEL), lambda i, j: (i, 0), pipeline_mode=pl.Buffered(1)),
                  pl.BlockSpec((1, D_MODEL), lambda i, j: (0, 0)),
                  mod_spec, mod_spec,
                  pl.BlockSpec((D_MODEL, tn), lambda i, j: (0, j))],
        out_specs=pl.BlockSpec((tm, tn), lambda i, j: (i, j)),
        out_shape=jax.ShapeDtypeStruct((T_ALL, n), F32),
        scratch_shapes=[pltpu.VMEM((tm, D_MODEL), BF16)],
        compiler_params=_cparams(("parallel", "arbitrary")),
        name=name,
    )(x, g.reshape(1, D_MODEL), sc, sh, w)


ODD_IN = 2 * SSD_DIM + 2 * SSD_BC + 2 * SSD_HEADS + 2 * GLA_QK_DIM + 2 * GLA_V_DIM + 2 * GLA_RANK
ODD_MAIN = ODD_IN - 2 * SSD_HEADS - 2 * GLA_RANK
ODD_COL_DT = SSD_DIM + SSD_CONV_CH
ODD_TAIL = ODD_MAIN
RELAYOUT_ROWS = 128


def _relayout_kernel(w_ref, tail_ref, main_ref, small_ref):
    lane = lax.broadcasted_iota(jnp.int32, (RELAYOUT_ROWS, 128), 1)
    lo = lane < 2 * SSD_HEADS
    main_ref[:, 0:SSD_DIM] = w_ref[:, 0:SSD_DIM].astype(BF16)
    main_ref[:, ODD_COL_XBC:ODD_MAIN] = w_ref[:, SSD_DIM:ODD_COL_DT].astype(BF16)
    first = w_ref[:, ODD_COL_DT:ODD_COL_DT + 128]
    small_ref[...] = jnp.where(lo, first, tail_ref[...]).astype(BF16)
    n_tiles = (ODD_TAIL - ODD_COL_DT) // 128
    prev = pltpu.roll(first, 64, 1)
    for t in range(n_tiles):
        if t + 1 < n_tiles:
            src = w_ref[:, ODD_COL_DT + 128 * (t + 1):ODD_COL_DT + 128 * (t + 2)]
        else:
            src = tail_ref[...]
        nxt = pltpu.roll(src, 64, 1)
        s = 128 * t
        dst = SSD_DIM + (2 * GLA_V_DIM + s if s < 2 * GLA_QK_DIM else s - 2 * GLA_QK_DIM)
        main_ref[:, dst:dst + 128] = jnp.where(lo, prev, nxt).astype(BF16)
        prev = nxt


def _relayout_odd_weights(w_in_odd, tail, j):
    return pl.pallas_call(
        _relayout_kernel,
        grid=(D_MODEL // RELAYOUT_ROWS,),
        in_specs=[pl.BlockSpec((None, RELAYOUT_ROWS, ODD_MAIN), lambda i: (j, i, 0)),
                  pl.BlockSpec((None, RELAYOUT_ROWS, 128), lambda i: (j, i, 0))],
        out_specs=[pl.BlockSpec((RELAYOUT_ROWS, ODD_MAIN), lambda i: (i, 0)),
                   pl.BlockSpec((RELAYOUT_ROWS, SMALL_W), lambda i: (i, 0))],
        out_shape=[jax.ShapeDtypeStruct((D_MODEL, ODD_MAIN), BF16), jax.ShapeDtypeStruct((D_MODEL, SMALL_W), BF16)],
        compiler_params=_cparams(("parallel",)),
        name="relayout_odd_weights",
    )(w_in_odd, tail)


def _outproj_kernel(a_ref, b_ref, w1_ref, w2_ref, x_ref, g_ref, o_ref):
    acc = _dot(a_ref[...], w1_ref[...].astype(BF16)) + _dot(b_ref[...], w2_ref[...].astype(BF16))
    o_ref[...] = x_ref[...] + g_ref[...] * acc


def _outproj(a, b, w, x, gate, name):
    k1 = a.shape[1]
    tm = PROJ_TM
    tn = 512
    return pl.pallas_call(
        _outproj_kernel,
        grid=(T_ALL // tm, D_MODEL // tn),
        in_specs=[pl.BlockSpec((tm, k1), lambda i, j: (i, 0)),
                  pl.BlockSpec((tm, k1), lambda i, j: (i, 0)),
                  pl.BlockSpec((k1, tn), lambda i, j: (0, j)),
                  pl.BlockSpec((k1, tn), lambda i, j: (1, j)),
                  pl.BlockSpec((tm, tn), lambda i, j: (i, j)),
                  pl.BlockSpec((None, 1, tn), lambda i, j: (_mod_row(i, tm), 0, j))],
        out_specs=pl.BlockSpec((tm, tn), lambda i, j: (i, j)),
        out_shape=jax.ShapeDtypeStruct((T_ALL, D_MODEL), F32),
        compiler_params=_cparams(("parallel", "parallel")),
        name=name,
    )(a, b, w, w, x, gate)


def _sgate_kernel(u_ref, v_ref, g_ref, ws_ref, bs_ref, o_ref):
    u = _gelu_tanh(u_ref[...])
    v = _gelu_tanh(v_ref[...])
    v = v * lax.rsqrt(jnp.mean(v * v, axis=-1, keepdims=True) + EPS) * g_ref[...]
    vb = v.astype(BF16)
    for c in range(TM // SG_CHUNK):
        rows = slice(c * SG_CHUNK, (c + 1) * SG_CHUNK)
        for g in range(SG_GROUPS):
            cols = slice(g * 128, (g + 1) * 128)
            s = _dot(ws_ref[g], vb[rows, cols]) + bs_ref[g]
            o_ref[rows, cols] = (u[rows, cols] * s).astype(o_ref.dtype)


def _spatial_gating(h, sg_norm, sg_w, sg_b):
    bias = jnp.broadcast_to(sg_b[:, :, None], (SG_GROUPS, SG_CHUNK, 128))
    return pl.pallas_call(
        _sgate_kernel,
        grid=(T_ALL // TM,),
        in_specs=[pl.BlockSpec((TM, SG_DIM), lambda i: (i, 0)),
                  pl.BlockSpec((TM, SG_DIM), lambda i: (i, 1)),
                  pl.BlockSpec((1, SG_DIM), lambda i: (0, 0)),
                  pl.BlockSpec((SG_GROUPS, SG_CHUNK, SG_CHUNK), lambda i: (0, 0, 0)),
                  pl.BlockSpec((SG_GROUPS, SG_CHUNK, 128), lambda i: (0, 0, 0))],
        out_specs=pl.BlockSpec((TM, SG_DIM), lambda i: (i, 0)),
        out_shape=jax.ShapeDtypeStruct((T_ALL, SG_DIM), BF16),
        compiler_params=_cparams(("parallel",)),
        name="spatial_gating",
    )(h, h, sg_norm.reshape(1, SG_DIM), sg_w.astype(BF16), bias)


def _softmax_rows(s):
    e = jnp.exp(s - jnp.max(s, axis=-1, keepdims=True))
    return e / jnp.sum(e, axis=-1, keepdims=True)


def _ctx_attn_kernel(*refs):
    q_ref, k_ref, v_ref = refs[:3]
    o_ref, kv_ref = refs[-2:]
    k = k_ref[...]
    v = v_ref[...]
    s = _dot_nt(q_ref[...].astype(BF16), k.astype(BF16)) * (NA_HEAD_DIM ** -0.5)
    p = _softmax_rows(s).astype(BF16)
    o_ref[...] = _dot(p, v.astype(BF16)).astype(o_ref.dtype)
    kv_ref[0, 0] = k
    kv_ref[1, 0] = v


def _context_attention(h, j, kv_prev):
    qc = 2 * SG_DIM // NA_HEAD_DIM
    in_specs = [pl.BlockSpec((L_CTX, NA_HEAD_DIM), lambda b, hh: (b, qc + hh)),
                pl.BlockSpec((L_CTX, NA_HEAD_DIM), lambda b, hh: (b, qc + NA_HEADS + hh)),
                pl.BlockSpec((L_CTX, NA_HEAD_DIM), lambda b, hh: (b, qc + 2 * NA_HEADS + hh))]
    args = [h, h, h]
    aliases = {}
    if kv_prev is not None:
        in_specs.append(_any_spec())
        args.append(kv_prev)
        aliases = {3: 1}
    return pl.pallas_call(
        _ctx_attn_kernel,
        grid=(N_CTX_SEQ, NA_HEADS),
        in_specs=in_specs,
        out_specs=[pl.BlockSpec((L_CTX, NA_HEAD_DIM), lambda b, hh: (b, hh)),
                   pl.BlockSpec((None, None, 2, 1, L_CTX, NA_HEAD_DIM), lambda b, hh: (b, j, 0, hh, 0, 0))],
        out_shape=[jax.ShapeDtypeStruct((T_ALL, NA_DIM), BF16),
                   jax.ShapeDtypeStruct((N_CTX_SEQ, N_EVEN, 2, NA_HEADS, L_CTX, NA_HEAD_DIM), F32)],
        input_output_aliases=aliases,
        compiler_params=_cparams(("parallel", "parallel")),
        name="context_attention",
    )(*args)


def _na_bias_table(rpb):
    col = np.arange(GRID_W)
    c0 = np.clip(col - WIN_C // 2, 0, GRID_W - WIN_C)
    col_ok = (col[None, :] >= c0[:, None]) & (col[None, :] < c0[:, None] + WIN_C)
    dc = np.clip(col[None, :] - col[:, None], -(WIN_C - 1), WIN_C - 1) + (WIN_C - 1)
    onehot = (dc.reshape(-1)[None, :] == np.arange(2 * WIN_C - 1)[:, None]).astype(np.float32)
    t = jnp.einsum("hrc,cn->hrn", rpb.astype(F32), onehot, precision=lax.Precision.HIGHEST)
    t = jnp.where(col_ok.reshape(-1), t, NEG_INF).reshape(NA_HEADS, 2 * WIN_R - 1, GRID_W, GRID_W)
    return jnp.stack([jnp.concatenate([t[:, d0 + w] for w in range(WIN_R)], axis=-1) for d0 in range(WIN_R)], axis=1)


def _na_attn_kernel(q_ref, k_ref, v_ref, kc_ref, vc_ref, bias_ref, o_in_ref, o_ref, kb, vb, kcb, vcb):
    del o_in_ref
    kb[...] = k_ref[...].astype(BF16)
    vb[...] = v_ref[...].astype(BF16)
    kcb[...] = kc_ref[...].astype(BF16)
    vcb[...] = vc_ref[...].astype(BF16)
    scale = NA_HEAD_DIM ** -0.5
    win = WIN_R * GRID_W

    def row(r, carry):
        r0 = jnp.clip(r - WIN_R // 2, 0, NA_ROWS - WIN_R)
        d0 = r0 - r + (WIN_R - 1)
        qs = pl.multiple_of(r * GRID_W, GRID_W)
        ks = pl.multiple_of(r0 * GRID_W, GRID_W)
        q = q_ref[pl.ds(qs, GRID_W), :].astype(BF16)
        s_loc = _dot_nt(q, kb[pl.ds(ks, win), :]) * scale + bias_ref[d0]
        s_ctx = _dot_nt(q, kcb[...]) * scale
        m = jnp.maximum(jnp.max(s_loc, axis=-1, keepdims=True), jnp.max(s_ctx, axis=-1, keepdims=True))
        e_loc = jnp.exp(s_loc - m)
        e_ctx = jnp.exp(s_ctx - m)
        den = jnp.sum(e_loc, axis=-1, keepdims=True) + jnp.sum(e_ctx, axis=-1, keepdims=True)
        o = _dot((e_loc / den).astype(BF16), vb[pl.ds(ks, win), :]) + _dot((e_ctx / den).astype(BF16), vcb[...])
        o_ref[pl.ds(qs, GRID_W), :] = o.astype(o_ref.dtype)
        return carry

    lax.fori_loop(0, NA_ROWS, row, 0, unroll=NA_UNROLL)


def _neighbourhood_attention(h, cache_kv, j, bias, o_prev):
    qc = 2 * SG_DIM // NA_HEAD_DIM
    rb = T_CTX // L_LAT
    cache_spec = lambda kv: pl.BlockSpec((None, None, None, None, PAST_LEN, NA_HEAD_DIM),
                                         lambda b, hh: (b, j, kv, hh, 0, 0))
    return pl.pallas_call(
        _na_attn_kernel,
        grid=(N_LAT_SEQ, NA_HEADS),
        in_specs=[pl.BlockSpec((L_LAT, NA_HEAD_DIM), lambda b, hh: (rb + b, qc + hh)),
                  pl.BlockSpec((L_LAT, NA_HEAD_DIM), lambda b, hh: (rb + b, qc + NA_HEADS + hh)),
                  pl.BlockSpec((L_LAT, NA_HEAD_DIM), lambda b, hh: (rb + b, qc + 2 * NA_HEADS + hh)),
                  cache_spec(0), cache_spec(1),
                  pl.BlockSpec((None, WIN_R, GRID_W, WIN_R * GRID_W), lambda b, hh: (hh, 0, 0, 0)),
                  _any_spec()],
        out_specs=pl.BlockSpec((L_LAT, NA_HEAD_DIM), lambda b, hh: (rb + b, hh)),
        out_shape=jax.ShapeDtypeStruct((T_ALL, NA_DIM), BF16),
        input_output_aliases={6: 0},
        scratch_shapes=[pltpu.VMEM((L_LAT, NA_HEAD_DIM), BF16), pltpu.VMEM((L_LAT, NA_HEAD_DIM), BF16),
                        pltpu.VMEM((PAST_LEN, NA_HEAD_DIM), BF16), pltpu.VMEM((PAST_LEN, NA_HEAD_DIM), BF16)],
        compiler_params=_cparams(("parallel", "parallel")),
        name="neighbourhood_attention",
    )(h, h, h, cache_kv, cache_kv, bias, o_prev)


CONV_ROWS = 256
CONV_COLS = 512


def _conv_kernel(transpose_out, x_ref, p_ref, n_ref, w_ref, b_ref, o_ref):
    i = pl.program_id(0)
    per = L_LAT // CONV_ROWS
    nct = T_CTX // CONV_ROWS
    pos = jnp.where(i < nct, 0, (i - nct) % per)
    last = jnp.where(i < nct, 0, per - 1)
    x = x_ref[...]
    prev_row = jnp.where(pos == 0, 0.0, p_ref[7:8, :])
    next_row = jnp.where(pos == last, 0.0, n_ref[0:1, :])
    ridx = lax.broadcasted_iota(jnp.int32, x.shape, 0)
    x_dn = jnp.where(ridx == 0, prev_row, pltpu.roll(x, 1, 0))
    x_up = jnp.where(ridx == CONV_ROWS - 1, next_row, pltpu.roll(x, CONV_ROWS - 1, 0))
    y = _silu(x_dn * w_ref[0:1, :] + x * w_ref[1:2, :] + x_up * w_ref[2:3, :] + b_ref[...])
    if transpose_out:
        for c in range(CONV_ROWS // SSD_CHUNK):
            o_ref[c] = y[c * SSD_CHUNK:(c + 1) * SSD_CHUNK, :].T
    else:
        o_ref[...] = y


def _conv_silu(h, conv_w, conv_b, col0, ncols, transpose_out):
    cb0 = (ODD_COL_XBC + col0) // CONV_COLS
    wb0 = col0 // CONV_COLS
    rb = CONV_ROWS // 8
    nblk8 = T_ALL // 8
    cpb = CONV_ROWS // SSD_CHUNK
    w8 = jnp.zeros((8, SSD_CONV_CH), F32).at[:3].set(conv_w)
    if transpose_out:
        out_spec = pl.BlockSpec((cpb, CONV_COLS, SSD_CHUNK), lambda i, j: (i, j, 0))
        out_shape = jax.ShapeDtypeStruct((T_ALL // SSD_CHUNK, ncols, SSD_CHUNK), F32)
    else:
        out_spec = pl.BlockSpec((CONV_ROWS, CONV_COLS), lambda i, j: (i, j))
        out_shape = jax.ShapeDtypeStruct((T_ALL, ncols), F32)
    return pl.pallas_call(
        functools.partial(_conv_kernel, transpose_out),
        grid=(T_ALL // CONV_ROWS, ncols // CONV_COLS),
        in_specs=[pl.BlockSpec((CONV_ROWS, CONV_COLS), lambda i, j: (i, cb0 + j)),
                  pl.BlockSpec((8, CONV_COLS), lambda i, j: (jnp.maximum(i * rb - 1, 0), cb0 + j)),
                  pl.BlockSpec((8, CONV_COLS), lambda i, j: (jnp.minimum((i + 1) * rb, nblk8 - 1), cb0 + j)),
                  pl.BlockSpec((8, CONV_COLS), lambda i, j: (0, wb0 + j)),
                  pl.BlockSpec((1, CONV_COLS), lambda i, j: (0, wb0 + j))],
        out_specs=out_spec,
        out_shape=out_shape,
        compiler_params=_cparams(("parallel", "parallel")),
        name="conv_silu_t" if transpose_out else "conv_silu",
    )(h, h, h, w8, conv_b.reshape(1, SSD_CONV_CH))


def _seq_info(t, chunk):
    nct = L_CTX // chunk
    ncl = L_LAT // chunk
    nctx = N_CTX_SEQ * nct
    is_ctx = t < nctx
    tl = jnp.maximum(t - nctx, 0)
    seq = jnp.where(is_ctx, t // nct, N_CTX_SEQ + tl // ncl)
    c = jnp.where(is_ctx, t % nct, tl % ncl)
    n = jnp.where(is_ctx, nct, ncl)
    first = jnp.where(is_ctx, (t // nct) * nct, nctx + (tl // ncl) * ncl)
    return seq, c, n, first


def _row_blk(t, chunk, rev):
    if not rev:
        return t
    _, c, n, first = _seq_info(t, chunk)
    return first + (n - 1 - c)


def _lat_seq(t, chunk):
    return jnp.maximum(_seq_info(t, chunk)[0] - N_CTX_SEQ, 0)


def _ctx_seq(t, chunk):
    return jnp.minimum(_seq_info(t, chunk)[0], N_CTX_SEQ - 1)


def _head_rows(mat, d, g):
    base = d * SSD_HEADS + g * SSD_GROUP_HEADS
    return jnp.concatenate(
        [jnp.broadcast_to(mat[base + r:base + r + 1, :], (SSD_HEAD_DIM, mat.shape[1])) for r in range(SSD_GROUP_HEADS)],
        axis=0)


def _ssd_kernel(rev, has_prev, *refs):
    refs = list(refs)
    xt_ref, bc_ref, sm_ref, dtb_ref, alog_ref, s0_ref = refs[:6]
    pos = 6
    if rev:
        yf_ref, z_ref, dcol_ref, gn_ref = refs[pos:pos + 4]
        pos += 4
    if has_prev:
        pos += 1
    y_ref, sfin_ref, s_scr = refs[pos:pos + 3]
    yt_scr = refs[pos + 3] if rev else None

    t = pl.program_id(0)
    seq, c, n, _ = _seq_info(t, SSD_CHUNK)
    is_ctx = seq < N_CTX_SEQ
    q = SSD_CHUNK
    d = 1 if rev else 0

    @pl.when(c == 0)
    def _():
        s_scr[...] = jnp.where(is_ctx, 0.0, s0_ref[...])

    ii = lax.broadcasted_iota(jnp.int32, (q, q), 0)
    jj = lax.broadcasted_iota(jnp.int32, (q, q), 1)
    mask = (jj >= ii) if rev else (jj <= ii)
    tri = mask.astype(BF16)

    dt = _softplus(sm_ref[...] + dtb_ref[...])
    dta = dt * (-jnp.exp(alog_ref[...]))
    cum = _mask_dot(tri, dta)
    tot = cum[0:1, :] if rev else cum[q - 1:q, :]
    cum_t = cum.T
    dt_t = dt.T
    te_t = (jnp.exp(tot - cum) * dt).T
    ecum_t = jnp.exp(cum_t)
    last = 0 if rev else q - 1
    etot_t = jnp.broadcast_to(ecum_t[:, last:last + 1], (q, q))

    for g in range(SSD_GROUPS):
        gs = slice(g * SSD_GROUP_DIM, (g + 1) * SSD_GROUP_DIM)
        bg = bc_ref[:, g * SSD_STATE:(g + 1) * SSD_STATE].astype(BF16)
        cg = bc_ref[:, SSD_BC + g * SSD_STATE:SSD_BC + (g + 1) * SSD_STATE].astype(BF16)
        cb = _dot_nt(cg, bg)
        s_g = s_scr[gs, :]
        x_g = xt_ref[gs, :]
        y_off = _dot_nt(s_g.astype(BF16), cg) * _head_rows(ecum_t, d, g)
        st = _dot((x_g * _head_rows(te_t, d, g)).astype(BF16), bg)
        s_scr[gs, :] = s_g * _head_rows(etot_t, d, g) + st
        for k in range(SSD_GROUP_HEADS // 2):
            w_pair = []
            for r in (2 * k, 2 * k + 1):
                col = d * SSD_HEADS + g * SSD_GROUP_HEADS + r
                seg = cum[:, col:col + 1] - cum_t[col:col + 1, :]
                w_pair.append((cb * jnp.where(mask, jnp.exp(seg), 0.0) * dt_t[col:col + 1, :]).astype(BF16))
            w2 = jnp.concatenate(w_pair, axis=1)
            ps = slice(k * 2 * SSD_HEAD_DIM, (k + 1) * 2 * SSD_HEAD_DIM)
            xp = x_g[ps, :].astype(BF16)
            rr = lax.broadcasted_iota(jnp.int32, xp.shape, 0)
            zero = jnp.zeros_like(xp)
            x2 = jnp.concatenate([jnp.where(rr < SSD_HEAD_DIM, xp, zero), jnp.where(rr >= SSD_HEAD_DIM, xp, zero)],
                                 axis=1)
            y_pair = _dot_nt(x2, w2) + y_off[ps, :]
            rows = slice(g * SSD_GROUP_DIM + k * 2 * SSD_HEAD_DIM, g * SSD_GROUP_DIM + (k + 1) * 2 * SSD_HEAD_DIM)
            if rev:
                yt_scr[rows, :] = y_pair
            else:
                y_ref[rows, :] = y_pair

    if rev:
        yt = yf_ref[...] + yt_scr[...] + dcol_ref[...] * xt_ref[...]
        y = yt.T * _silu(z_ref[...])
        y = y * lax.rsqrt(jnp.mean(y * y, axis=-1, keepdims=True) + EPS) * gn_ref[...]
        y_ref[...] = y.astype(y_ref.dtype)

    @pl.when((c == n - 1) & is_ctx)
    def _():
        sfin_ref[...] = s_scr[...]


def _ssd_scan(rev, layer, xt, bc, small, dtb, alog, s0, states_prev, extra=None):
    q = SSD_CHUNK
    d = 1 if rev else 0
    blk = lambda t: _row_blk(t, q, rev)
    in_specs = [pl.BlockSpec((None, SSD_DIM, q), lambda t: (blk(t), 0, 0)),
                pl.BlockSpec((q, 2 * SSD_BC), lambda t: (blk(t), 0)),
                pl.BlockSpec((q, SMALL_W), lambda t: (blk(t), 0)),
                pl.BlockSpec((1, SMALL_W), lambda t: (0, 0)),
                pl.BlockSpec((1, SMALL_W), lambda t: (0, 0)),
                pl.BlockSpec((None, SSD_DIM, SSD_STATE), lambda t: (_lat_seq(t, q), 0, 0))]
    args = [xt, bc, small, dtb, alog, s0]
    if rev:
        yf, h, dcol, gn = extra
        in_specs += [pl.BlockSpec((None, SSD_DIM, q), lambda t: (blk(t), 0, 0)),
                     pl.BlockSpec((q, SSD_DIM), lambda t: (blk(t), 0)),
                     pl.BlockSpec((SSD_DIM, q), lambda t: (0, 0)),
                     pl.BlockSpec((1, SSD_DIM), lambda t: (0, 0))]
        args += [yf, h, dcol, gn]
        y_spec = pl.BlockSpec((q, SSD_DIM), lambda t: (blk(t), 0))
        y_shape = jax.ShapeDtypeStruct((T_ALL, SSD_DIM), BF16)
    else:
        y_spec = pl.BlockSpec((None, SSD_DIM, q), lambda t: (blk(t), 0, 0))
        y_shape = jax.ShapeDtypeStruct((T_ALL // q, SSD_DIM, q), F32)
    aliases = {}
    if states_prev is not None:
        aliases = {len(args): 1}
        in_specs.append(_any_spec())
        args.append(states_prev)
    scratch = [pltpu.VMEM((SSD_DIM, SSD_STATE), F32)]
    if rev:
        scratch.append(pltpu.VMEM((SSD_DIM, q), F32))
    return pl.pallas_call(
        functools.partial(_ssd_kernel, rev, states_prev is not None),
        grid=(T_ALL // q,),
        in_specs=in_specs,
        out_specs=[y_spec,
                   pl.BlockSpec((None, None, None, SSD_DIM, SSD_STATE), lambda t: (_ctx_seq(t, q), layer, d, 0, 0))],
        out_shape=[y_shape, jax.ShapeDtypeStruct((N_CTX_SEQ, N_ODD, 2, SSD_DIM, SSD_STATE), F32)],
        input_output_aliases=aliases,
        scratch_shapes=scratch,
        compiler_params=_cparams(("arbitrary",)),
        name="ssd_bwd" if rev else "ssd_fwd",
    )(*args)


def _gla_kernel(rev, has_prev, *refs):
    refs = list(refs)
    q_ref, k_ref, v_ref, sm_ref, wa_ref, ba_ref, s0_ref = refs[:7]
    pos = 7
    if rev:
        of_ref, gate_ref, gn_ref = refs[pos:pos + 3]
        pos += 3
    if has_prev:
        pos += 1
    o_ref, sfin_ref, s_scr = refs[pos:pos + 3]

    t = pl.program_id(0)
    seq, c, n, _ = _seq_info(t, GLA_CHUNK)
    is_ctx = seq < N_CTX_SEQ
    cq = GLA_CHUNK

    @pl.when(c == 0)
    def _():
        for hh in range(GLA_HEADS):
            s_scr[hh] = jnp.where(is_ctx, 0.0, s0_ref[hh].T)

    ii = lax.broadcasted_iota(jnp.int32, (cq, cq), 0)
    jj = lax.broadcasted_iota(jnp.int32, (cq, cq), 1)
    mask = (jj >= ii) if rev else (jj <= ii)
    tri = mask.astype(BF16)

    g_lin = _dot(sm_ref[...].astype(BF16), wa_ref[...]) + ba_ref[...]
    g = -_softplus(-g_lin) / GLA_GATE_NORM
    cum = _mask_dot(tri, g)
    last = cum[0:1, :] if rev else cum[cq - 1:cq, :]
    q_e = (q_ref[...] * (GLA_HEAD_K ** -0.5)) * jnp.exp(cum)
    k = k_ref[...]
    k_e = k * jnp.exp(-cum)
    k_end = k * jnp.exp(last - cum)
    e_last = jnp.exp(last)

    for hh in range(GLA_HEADS):
        ks = slice(hh * GLA_HEAD_K, (hh + 1) * GLA_HEAD_K)
        vs = slice(hh * GLA_HEAD_V, (hh + 1) * GLA_HEAD_V)
        qh = q_e[:, ks].astype(BF16)
        vh = v_ref[:, vs]
        vh_b = vh.astype(BF16)
        s_prev = s_scr[hh]
        att = jnp.where(mask, _dot_nt(qh, k_e[:, ks].astype(BF16)), 0.0)
        o = _dot(att.astype(BF16), vh_b) + _dot_nt(qh, s_prev.astype(BF16))
        st = _dot(vh.T.astype(BF16), k_end[:, ks].astype(BF16))
        s_scr[hh] = s_prev * e_last[:, ks] + st
        if rev:
            o = o + of_ref[:, vs]
            o = o * lax.rsqrt(jnp.mean(o * o, axis=-1, keepdims=True) + EPS) * gn_ref[...]
            o_ref[:, vs] = (o * _silu(gate_ref[:, vs])).astype(o_ref.dtype)
        else:
            o_ref[:, vs] = o

    @pl.when((c == n - 1) & is_ctx)
    def _():
        for hh in range(GLA_HEADS):
            sfin_ref[hh] = s_scr[hh].T


def _gla_scan(rev, layer, h, small, wa, ba, s0, states_prev, extra=None):
    cq = GLA_CHUNK
    d = 1 if rev else 0
    cv = SSD_DIM // GLA_V_DIM
    cqk = (SSD_DIM + 2 * GLA_V_DIM) // GLA_QK_DIM
    row = lambda t: _row_blk(t, cq, rev)
    in_specs = [pl.BlockSpec((cq, GLA_QK_DIM), lambda t: (row(t), cqk)),
                pl.BlockSpec((cq, GLA_QK_DIM), lambda t: (row(t), cqk + 1)),
                pl.BlockSpec((cq, GLA_V_DIM), lambda t: (row(t), cv)),
                pl.BlockSpec((cq, SMALL_W), lambda t: (row(t), 0)),
                pl.BlockSpec((SMALL_W, GLA_QK_DIM), lambda t: (0, 0)),
                pl.BlockSpec((1, GLA_QK_DIM), lambda t: (0, 0)),
                pl.BlockSpec((None, GLA_HEADS, GLA_HEAD_K, GLA_HEAD_V), lambda t: (_lat_seq(t, cq), 0, 0, 0))]
    args = [h, h, h, small, wa, ba, s0]
    if rev:
        o_f, gn = extra
        in_specs += [pl.BlockSpec((cq, GLA_V_DIM), lambda t: (row(t), 0)),
                     pl.BlockSpec((cq, GLA_V_DIM), lambda t: (row(t), cv + 1)),
                     pl.BlockSpec((1, GLA_HEAD_V), lambda t: (0, 0))]
        args += [o_f, h, gn]
    aliases = {}
    if states_prev is not None:
        aliases = {len(args): 1}
        in_specs.append(_any_spec())
        args.append(states_prev)
    return pl.pallas_call(
        functools.partial(_gla_kernel, rev, states_prev is not None),
        grid=(T_ALL // cq,),
        in_specs=in_specs,
        out_specs=[pl.BlockSpec((cq, GLA_V_DIM), lambda t: (row(t), 0)),
                   pl.BlockSpec((None, None, None, GLA_HEADS, GLA_HEAD_K, GLA_HEAD_V),
                                lambda t: (_ctx_seq(t, cq), layer, d, 0, 0, 0))],
        out_shape=[jax.ShapeDtypeStruct((T_ALL, GLA_V_DIM), BF16 if rev else F32),
                   jax.ShapeDtypeStruct((N_CTX_SEQ, N_ODD, 2, GLA_HEADS, GLA_HEAD_K, GLA_HEAD_V), F32)],
        input_output_aliases=aliases,
        scratch_shapes=[pltpu.VMEM((GLA_HEADS, GLA_HEAD_V, GLA_HEAD_K), F32)],
        compiler_params=_cparams(("arbitrary",)),
        name="gla_bwd" if rev else "gla_fwd",
    )(*args)


ROUTER_ROWS = 32


def _first_max(vals):
    out = []
    for a, va in enumerate(vals):
        ok = None
        for b, vb in enumerate(vals):
            if a == b:
                continue
            t = (va > vb) if b < a else (va >= vb)
            ok = t if ok is None else (ok & t)
        out.append(ok)
    return out


def _router_kernel(x_ref, g_ref, sc_ref, sh_ref, whi_ref, wlo_ref, bsel_ref, tri_ref,
                   hn_ref, gates_ref, rank_ref, cnt_ref, carry):
    i = pl.program_id(0)

    @pl.when(i == 0)
    def _():
        carry[...] = jnp.zeros_like(carry)

    hn = _norm_mod(x_ref[...], g_ref[...], sc_ref[...], sh_ref[...])
    hn_ref[...] = hn
    hb = hn.astype(BF16)
    hl = (hn - hb.astype(F32)).astype(BF16)
    logits = _dot(hb, whi_ref[...]) + _dot(hb, wlo_ref[...]) + _dot(hl, whi_ref[...])
    lt = logits.T
    score = [jax.nn.sigmoid(lt[8 * m:8 * m + 8, :]) for m in range(EXPERTS_PER_GROUP)]
    sel = [score[m] + bsel_ref[8 * m:8 * m + 8, :] for m in range(EXPERTS_PER_GROUP)]

    hi1, lo1 = jnp.maximum(sel[0], sel[1]), jnp.minimum(sel[0], sel[1])
    hi2, lo2 = jnp.maximum(sel[2], sel[3]), jnp.minimum(sel[2], sel[3])
    grp = jnp.maximum(hi1, hi2) + jnp.maximum(jnp.minimum(hi1, hi2), jnp.maximum(lo1, lo2))
    best = _first_max([grp[g:g + 1, :] for g in range(N_EXPERT_GROUPS)])
    srow = lax.broadcasted_iota(jnp.int32, grp.shape, 0)
    gmask = jnp.zeros(grp.shape, jnp.bool_)
    for g in range(N_EXPERT_GROUPS):
        gmask = gmask | ((srow == g) & best[g])

    first = _first_max(sel)
    sel2 = [jnp.where(first[m], -jnp.inf, sel[m]) for m in range(EXPERTS_PER_GROUP)]
    second = _first_max(sel2)
    chosen = [(first[m] | second[m]) & gmask for m in range(EXPERTS_PER_GROUP)]
    wsum = sum(jnp.where(chosen[m], score[m], 0.0) for m in range(EXPERTS_PER_GROUP))
    wtot = jnp.sum(wsum, axis=0, keepdims=True)
    cm = jnp.concatenate([chosen[m].astype(F32) for m in range(EXPERTS_PER_GROUP)], axis=0)
    sc_all = jnp.concatenate(score, axis=0)
    gates_ref[...] = jnp.where(cm > 0, sc_all / wtot, 0.0)
    within = _dot(cm.astype(BF16), tri_ref[...])
    rank_ref[...] = jnp.where(cm > 0, carry[:, 0:1] + within, -1.0)
    carry[...] = carry[...] + jnp.sum(cm, axis=1, keepdims=True)
    cnt_ref[...] = carry[...]


def _router(x, g, sc, sh, w_router, b_router):
    slot_of_expert = np.array([(e % EXPERTS_PER_GROUP) * 8 + e // EXPERTS_PER_GROUP for e in range(N_EXPERTS)])
    w128 = jnp.zeros((D_MODEL, 128), F32).at[:, slot_of_expert].set(w_router)
    w_hi = w128.astype(BF16)
    w_lo = (w128 - w_hi.astype(F32)).astype(BF16)
    b32 = jnp.zeros((ROUTER_ROWS,), F32).at[slot_of_expert].set(b_router.astype(F32))
    bsel = jnp.broadcast_to(b32[:, None], (ROUTER_ROWS, TM))
    tri = jnp.asarray(np.triu(np.ones((TM, TM), np.float32), 1), BF16)
    mod_spec = pl.BlockSpec((None, 1, D_MODEL), lambda i: (_mod_row(i, TM), 0, 0))
    hn, gates, rank, cnt = pl.pallas_call(
        _router_kernel,
        grid=(T_ALL // TM,),
        in_specs=[pl.BlockSpec((TM, D_MODEL), lambda i: (i, 0)),
                  pl.BlockSpec((1, D_MODEL), lambda i: (0, 0)),
                  mod_spec, mod_spec,
                  pl.BlockSpec((D_MODEL, 128), lambda i: (0, 0)),
                  pl.BlockSpec((D_MODEL, 128), lambda i: (0, 0)),
                  pl.BlockSpec((ROUTER_ROWS, TM), lambda i: (0, 0)),
                  pl.BlockSpec((TM, TM), lambda i: (0, 0))],
        out_specs=[pl.BlockSpec((TM, D_MODEL), lambda i: (i, 0)),
                   pl.BlockSpec((ROUTER_ROWS, TM), lambda i: (0, i)),
                   pl.BlockSpec((ROUTER_ROWS, TM), lambda i: (0, i)),
                   pl.BlockSpec((ROUTER_ROWS, 128), lambda i: (0, 0))],
        out_shape=[jax.ShapeDtypeStruct((T_ALL, D_MODEL), F32),
                   jax.ShapeDtypeStruct((ROUTER_ROWS, T_ALL), F32),
                   jax.ShapeDtypeStruct((ROUTER_ROWS, T_ALL), F32),
                   jax.ShapeDtypeStruct((ROUTER_ROWS, 128), F32)],
        scratch_shapes=[pltpu.VMEM((ROUTER_ROWS, 128), F32)],
        compiler_params=_cparams(("arbitrary",)),
        name="router",
    )(x, g.reshape(1, D_MODEL), sc, sh, w_hi, w_lo, bsel, tri)
    return hn, gates[slot_of_expert], rank[slot_of_expert], cnt[slot_of_expert, 0]


def _dispatch_plan(gates, rank, cnt):
    cnt = cnt.astype(jnp.int32)
    padded = ((cnt + MOE_TM - 1) // MOE_TM) * MOE_TM
    end = jnp.cumsum(padded)
    start = end - padded
    chosen = rank >= 0
    slot = start[:, None] + rank.astype(jnp.int32)
    slot_a = jnp.min(jnp.where(chosen, slot, MOE_ROWS), axis=0)
    slot_b = jnp.max(jnp.where(chosen, slot, -1), axis=0)
    w_a = jnp.sum(jnp.where(chosen & (slot == slot_a[None]), gates, 0.0), axis=0)
    w_b = jnp.sum(jnp.where(chosen & (slot == slot_b[None]), gates, 0.0), axis=0)
    tile_row = jnp.arange(MOE_TILES, dtype=jnp.int32) * MOE_TM
    tile_expert = jnp.minimum(jnp.sum(tile_row[:, None] >= end[None, :], axis=1), N_EXPERTS - 1).astype(jnp.int32)
    changed = jnp.concatenate([jnp.ones((1,), jnp.bool_), tile_expert[1:] != tile_expert[:-1]])
    tile_state = jnp.where(tile_row < end[-1], 1 + changed.astype(jnp.int32), 0).astype(jnp.int32)
    tidx = jnp.arange(MOE_TILES, dtype=jnp.int32)
    later_first = (tidx[None, :] > tidx[:, None]) & (tile_state[None, :] == 2)
    nxt = jnp.min(jnp.where(later_first, tidx[None, :], MOE_TILES), axis=1)
    next_expert = jnp.where(nxt < MOE_TILES, tile_expert[jnp.minimum(nxt, MOE_TILES - 1)], -1).astype(jnp.int32)
    return slot_a, slot_b, w_a, w_b, tile_expert, tile_state, next_expert


def _dispatch_kernel(sa_ref, sb_ref, hn_ref, xs_in_ref, xs_ref, sem):
    del xs_in_ref
    base = pl.program_id(0) * TM

    def row_copies(r):
        src = hn_ref.at[pl.ds(r, 1)]
        return (pltpu.make_async_copy(src, xs_ref.at[pl.ds(sa_ref[base + r], 1)], sem),
                pltpu.make_async_copy(src, xs_ref.at[pl.ds(sb_ref[base + r], 1)], sem))

    def start(r, carry):
        for cp in row_copies(r):
            cp.start()
        return carry

    def wait(r, carry):
        for cp in row_copies(r):
            cp.wait()
        return carry

    lax.fori_loop(0, TM, start, 0, unroll=DISPATCH_UNROLL)
    lax.fori_loop(0, TM, wait, 0, unroll=DISPATCH_UNROLL)


def _dispatch(hn, slot_a, slot_b, xs0):
    return pl.pallas_call(
        _dispatch_kernel,
        grid_spec=pltpu.PrefetchScalarGridSpec(
            num_scalar_prefetch=2,
            grid=(T_ALL // TM,),
            in_specs=[pl.BlockSpec((TM, D_MODEL), lambda i, sa, sb: (i, 0)), _any_spec()],
            out_specs=_any_spec(),
            scratch_shapes=[pltpu.SemaphoreType.DMA(())]),
        out_shape=jax.ShapeDtypeStruct((MOE_ROWS, D_MODEL), F32),
        input_output_aliases={3: 0},
        compiler_params=_cparams(("arbitrary",)),
        name="moe_dispatch",
    )(slot_a, slot_b, hn, xs0)


def _moe_kernel(layer, te_ref, ts_ref, nx_ref, x_ref, wg_hbm, wu_hbm, wd_hbm, o_ref,
                wg_f, wu_f, wd_f, wg_b, wu_b, wd_b, sem):
    i = pl.program_id(0)
    state = ts_ref[i]

    def weight_copies(e):
        return (pltpu.make_async_copy(wg_hbm.at[layer, e], wg_f, sem.at[0]),
                pltpu.make_async_copy(wu_hbm.at[layer, e], wu_f, sem.at[1]),
                pltpu.make_async_copy(wd_hbm.at[layer, e], wd_f, sem.at[2]))

    @pl.when(i == 0)
    def _():
        for cp in weight_copies(te_ref[0]):
            cp.start()

    @pl.when(state == 2)
    def _():
        cg, cu, cd = weight_copies(te_ref[i])
        cg.wait()
        wg_b[...] = wg_f[...].astype(BF16)
        cu.wait()
        wu_b[...] = wu_f[...].astype(BF16)
        cd.wait()
        wd_b[...] = wd_f[...].astype(BF16)

        @pl.when(nx_ref[i] >= 0)
        def _():
            for cp in weight_copies(nx_ref[i]):
                cp.start()

    @pl.when(state > 0)
    def _():
        x = x_ref[...].astype(BF16)
        act = (_silu(_dot(x, wg_b[...])) * _dot(x, wu_b[...])).astype(BF16)
        o_ref[...] = _dot(act, wd_b[...])

    @pl.when(state == 0)
    def _():
        o_ref[...] = jnp.zeros_like(o_ref)


def _moe_experts(layer, tile_expert, tile_state, next_expert, xs, w_gate, w_up, w_down):
    return pl.pallas_call(
        functools.partial(_moe_kernel, layer),
        grid_spec=pltpu.PrefetchScalarGridSpec(
            num_scalar_prefetch=3,
            grid=(MOE_TILES,),
            in_specs=[pl.BlockSpec((MOE_TM, D_MODEL), lambda i, te, ts, nx: (i, 0)),
                      _any_spec(), _any_spec(), _any_spec()],
            out_specs=pl.BlockSpec((MOE_TM, D_MODEL), lambda i, te, ts, nx: (i, 0)),
            scratch_shapes=[pltpu.VMEM((D_MODEL, D_FF), F32), pltpu.VMEM((D_MODEL, D_FF), F32),
                            pltpu.VMEM((D_FF, D_MODEL), F32),
                            pltpu.VMEM((D_MODEL, D_FF), BF16), pltpu.VMEM((D_MODEL, D_FF), BF16),
                            pltpu.VMEM((D_FF, D_MODEL), BF16),
                            pltpu.SemaphoreType.DMA((3,))]),
        out_shape=jax.ShapeDtypeStruct((MOE_ROWS, D_MODEL), F32),
        compiler_params=_cparams(("arbitrary",)),
        name="moe_experts",
    )(tile_expert, tile_state, next_expert, xs, w_gate, w_up, w_down)


def _combine_kernel(x_ref, g_ref, ya_ref, yb_ref, wa_ref, wb_ref, o_ref):
    o_ref[...] = x_ref[...] + g_ref[...] * (wa_ref[...] * ya_ref[...] + wb_ref[...] * yb_ref[...])


def _combine(x, gate, ya, yb, w_a, w_b):
    row = pl.BlockSpec((TM, D_MODEL), lambda i: (i, 0))
    wspec = pl.BlockSpec((TM, 1), lambda i: (i, 0))
    return pl.pallas_call(
        _combine_kernel,
        grid=(T_ALL // TM,),
        in_specs=[row, pl.BlockSpec((None, 1, D_MODEL), lambda i: (_mod_row(i, TM), 0, 0)), row, row, wspec, wspec],
        out_specs=row,
        out_shape=jax.ShapeDtypeStruct((T_ALL, D_MODEL), F32),
        compiler_params=_cparams(("parallel",)),
        name="moe_combine",
    )(x, gate, ya, yb, w_a.reshape(T_ALL, 1), w_b.reshape(T_ALL, 1))


def _take_rows(a, idx):
    return a.at[idx].get(mode="promise_in_bounds")


def _moe_layer(layer, x, xs_buf, g, sc, sh, gate, w_router, b_router, w_gate, w_up, w_down):
    hn, gates, rank, cnt = _router(x, g, sc, sh, w_router, b_router)
    slot_a, slot_b, w_a, w_b, tile_expert, tile_state, next_expert = _dispatch_plan(gates, rank, cnt)
    xs = _dispatch(hn, slot_a, slot_b, xs_buf)
    y = _moe_experts(layer, tile_expert, tile_state, next_expert, xs, w_gate, w_up, w_down)
    return _combine(x, gate, _take_rows(y, slot_a), _take_rows(y, slot_b), w_a, w_b), xs


def _final_norm_kernel(x_ref, g_ref, oc_ref, ol_ref):
    x = x_ref[...]
    y = x * lax.rsqrt(jnp.mean(x * x, axis=-1, keepdims=True) + EPS) * g_ref[...]
    is_ctx = pl.program_id(0) < T_CTX // TM

    @pl.when(is_ctx)
    def _():
        oc_ref[...] = y

    @pl.when(jnp.logical_not(is_ctx))
    def _():
        ol_ref[...] = y


def _final_norm(x, g):
    nct = T_CTX // TM
    return pl.pallas_call(
        _final_norm_kernel,
        grid=(T_ALL // TM,),
        in_specs=[pl.BlockSpec((TM, D_MODEL), lambda i: (i, 0)), pl.BlockSpec((1, D_MODEL), lambda i: (0, 0))],
        out_specs=[pl.BlockSpec((TM, D_MODEL), lambda i: (jnp.minimum(i, nct - 1), 0)),
                   pl.BlockSpec((TM, D_MODEL), lambda i: (jnp.maximum(i - nct, 0), 0))],
        out_shape=[jax.ShapeDtypeStruct((T_CTX, D_MODEL), F32), jax.ShapeDtypeStruct((T_LAT, D_MODEL), F32)],
        compiler_params=_cparams(("arbitrary",)),
        name="final_norm",
    )(x, g.reshape(1, D_MODEL))


def _even_layer(x, mods, j, cache_kv, kv_new, norm_g, w_in, w_out, sg_norm, sg_w, sg_b, rpb):
    sh1, sc1, g1 = mods
    h = _inproj(x, norm_g, sc1, sh1, w_in, "inproj_even")
    a_out = _spatial_gating(h, sg_norm, sg_w, sg_b)
    o, kv_new = _context_attention(h, j, kv_new)
    o = _neighbourhood_attention(h, cache_kv, j, _na_bias_table(rpb), o)
    return _outproj(a_out, o, w_out, x, g1, "outproj_even"), kv_new


def _odd_layer(x, mods, j, state_ssm, state_gla, ssm_new, gla_new, norm_g, w_in, w_tail, w_out, conv_w, conv_b,
               dt_bias, a_log, ssd_d, ssd_norm, w_a2, b_a, gla_norm):
    sh1, sc1, g1 = mods
    w_main, w_small = _relayout_odd_weights(w_in, w_tail, j)
    h = _inproj(x, norm_g, sc1, sh1, w_main, "inproj_odd")
    small = _inproj(x, norm_g, sc1, sh1, w_small, "inproj_odd_small")

    xt = _conv_silu(h, conv_w, conv_b, 0, SSD_DIM, True)
    bc = _conv_silu(h, conv_w, conv_b, SSD_DIM, 2 * SSD_BC, False)
    pad = jnp.zeros((SMALL_W - 2 * SSD_HEADS,), F32)
    dtb = jnp.concatenate([dt_bias.reshape(-1).astype(F32), pad]).reshape(1, SMALL_W)
    alog = jnp.concatenate([a_log.reshape(-1).astype(F32), pad]).reshape(1, SMALL_W)
    s0 = state_ssm.reshape(N_LAT_SEQ, 2, SSD_DIM, SSD_STATE)
    y_f, ssm_new = _ssd_scan(False, j, xt, bc, small, dtb, alog, s0[:, 0], ssm_new)
    dcol = jnp.broadcast_to(jnp.repeat(ssd_d.astype(F32), SSD_HEAD_DIM)[:, None], (SSD_DIM, SSD_CHUNK))
    y_c, ssm_new = _ssd_scan(True, j, xt, bc, small, dtb, alog, s0[:, 1], ssm_new,
                             (y_f, h, dcol, ssd_norm.reshape(1, SSD_DIM)))

    def wa_pad(dd):
        lo = 2 * SSD_HEADS + dd * GLA_RANK
        return jnp.zeros((SMALL_W, GLA_QK_DIM), F32).at[lo:lo + GLA_RANK].set(w_a2[dd]).astype(BF16)

    o_f, gla_new = _gla_scan(False, j, h, small, wa_pad(0), b_a[0].reshape(1, GLA_QK_DIM), state_gla[:, 0], gla_new)
    y_d, gla_new = _gla_scan(True, j, h, small, wa_pad(1), b_a[1].reshape(1, GLA_QK_DIM), state_gla[:, 1], gla_new,
                             (o_f, gla_norm.reshape(1, GLA_HEAD_V)))

    return _outproj(y_c, y_d, w_out, x, g1, "outproj_odd"), ssm_new, gla_new


def kernel(x_prompt, x_sample, cache_kv, state_ssm, state_gla, c, c_ctx, w_mod, b_mod, norm_mix, norm_ffn,
           norm_final, w_in_even, w_out_even, sg_norm, sg_w, sg_b, na_rpb, w_in_odd, w_out_odd, ssd_conv_w,
           ssd_conv_b, ssd_dt_bias, ssd_a_log, ssd_d, ssd_norm, gla_w_a2, gla_b_a, gla_norm, w_router, b_router,
           w_gate, w_up, w_down):
    x = jnp.concatenate([x_prompt.reshape(T_CTX, D_MODEL), x_sample.reshape(T_LAT, D_MODEL)], axis=0)
    c_rows = jnp.concatenate([c_ctx[None, :], c, jnp.zeros((N_MOD_ROWS - 1 - N_LAT_SEQ, D_MODEL), F32)], axis=0)
    mod = _modulation(c_rows, w_mod, b_mod)

    kv_new, ssm_new, gla_new = None, None, None
    xs_buf = jnp.zeros((MOE_ROWS, D_MODEL), F32)
    w_tail = jnp.pad(w_in_odd[:, :, ODD_TAIL:], ((0, 0), (0, 0), (0, 128 - (ODD_IN - ODD_TAIL))))
    for i in range(DEPTH):
        m = mod[i, :1 + N_LAT_SEQ].reshape(1 + N_LAT_SEQ, N_MOD, 1, D_MODEL)
        sh1, sc1, g1, sh2, sc2, g2 = [m[:, s] for s in range(N_MOD)]
        j = i // 2
        if i % 2 == 0:
            x, kv_new = _even_layer(x, (sh1, sc1, g1), j, cache_kv, kv_new, norm_mix[i], w_in_even[j],
                                    w_out_even[j], sg_norm[j], sg_w[j], sg_b[j], na_rpb[j])
        else:
            x, ssm_new, gla_new = _odd_layer(x, (sh1, sc1, g1), j, state_ssm[:, j], state_gla[:, j], ssm_new,
                                             gla_new, norm_mix[i], w_in_odd, w_tail, w_out_odd[j], ssd_conv_w[j],
                                             ssd_conv_b[j], ssd_dt_bias[j], ssd_a_log[j], ssd_d[j], ssd_norm[j],
                                             gla_w_a2[j], gla_b_a[j], gla_norm[j])
        x, xs_buf = _moe_layer(i, x, xs_buf, norm_ffn[i], sc2, sh2, g2, w_router, b_router, w_gate, w_up, w_down)

    y_ctx, y_lat = _final_norm(x, norm_final)
    return (y_ctx.reshape(N_CTX_SEQ, L_CTX, D_MODEL),
            y_lat.reshape(N_LAT_SEQ, L_LAT, D_MODEL),
            kv_new,
            ssm_new.reshape(N_CTX_SEQ, N_ODD, 2, SSD_HEADS, SSD_HEAD_DIM, SSD_STATE),
            gla_new)
```

```python
import functools

import numpy as np
import jax
import jax.numpy as jnp
from jax import lax
from jax.experimental import pallas as pl
from jax.experimental.pallas import tpu as pltpu

F32 = jnp.float32
BF16 = jnp.bfloat16

D_MODEL = 2048
DEPTH = 4
N_EVEN = 2
N_ODD = 2
N_MOD = 6
EPS = 1e-6
NEG_INF = -1e30

N_CTX_SEQ = 16
L_CTX = 256
N_LAT_SEQ = 2
L_LAT = 2048
PAST_LEN = 512
T_CTX = N_CTX_SEQ * L_CTX
T_LAT = N_LAT_SEQ * L_LAT
T_ALL = T_CTX + T_LAT
N_MOD_ROWS = 8

GRID_W = 64
SG_DIM = 1024
SG_GROUPS = 8
SG_CHUNK = 128
NA_DIM = 1024
NA_HEADS = 8
NA_HEAD_DIM = 128
WIN_R = 8
WIN_C = 16
NA_ROWS = L_LAT // GRID_W
NA_UNROLL = 4

SSD_DIM = 2048
SSD_HEADS = 32
SSD_HEAD_DIM = 64
SSD_GROUPS = 4
SSD_GROUP_HEADS = SSD_HEADS // SSD_GROUPS
SSD_GROUP_DIM = SSD_GROUP_HEADS * SSD_HEAD_DIM
SSD_STATE = 128
SSD_CHUNK = 128
SSD_BC = SSD_GROUPS * SSD_STATE
SSD_CONV_CH = SSD_DIM + 2 * SSD_BC
GLA_HEADS = 4
GLA_QK_DIM = 1024
GLA_V_DIM = 2048
GLA_HEAD_K = 256
GLA_HEAD_V = 512
GLA_RANK = 16
GLA_GATE_NORM = 16.0
GLA_CHUNK = 64
SMALL_W = 128
ODD_COL_XBC = SSD_DIM + 2 * GLA_V_DIM + 2 * GLA_QK_DIM

N_EXPERTS = 16
N_EXPERT_GROUPS = 4
EXPERTS_PER_GROUP = 4
D_FF = 1024
MOE_TM = 256
MOE_TILES = (2 * T_ALL) // MOE_TM + N_EXPERTS
MOE_ROWS = MOE_TILES * MOE_TM

DISPATCH_UNROLL = 8
TM = 512
PROJ_TM = 1024
VMEM_LIMIT = 56 * 1024 * 1024


def _cparams(sem):
    return pltpu.CompilerParams(dimension_semantics=sem, vmem_limit_bytes=VMEM_LIMIT)


def _mod_row(i, tm):
    nct = T_CTX // tm
    per = L_LAT // tm
    return jnp.where(i < nct, 0, 1 + (i - nct) // per)


def _silu(x):
    return x * jax.nn.sigmoid(x)


def _softplus(x):
    return jnp.maximum(x, 0.0) + jnp.log1p(jnp.exp(-jnp.abs(x)))


def _gelu_tanh(x):
    return 0.5 * x * (1.0 + jnp.tanh(np.sqrt(2.0 / np.pi).astype(np.float32) * (x + 0.044715 * (x * x * x))))


def _split3(x):
    hi = x.astype(BF16)
    r1 = x - hi.astype(F32)
    mid = r1.astype(BF16)
    lo = (r1 - mid.astype(F32)).astype(BF16)
    return hi, mid, lo


def _dot(a, b):
    return jnp.dot(a, b, preferred_element_type=F32)


def _dot_nt(a, b):
    return lax.dot_general(a, b, (((1,), (1,)), ((), ())), preferred_element_type=F32)


def _mask_dot(mask_bf16, x):
    hi, mid, lo = _split3(x)
    return _dot(mask_bf16, hi) + _dot(mask_bf16, mid) + _dot(mask_bf16, lo)


def _any_spec():
    return pl.BlockSpec(memory_space=pl.ANY)


def _mod_kernel(c_ref, w_ref, b_ref, o_ref):
    s = _silu(c_ref[...]).astype(BF16)
    o_ref[...] = _dot(s, w_ref[...].astype(BF16)) + b_ref[...]


def _modulation(c_rows, w_mod, b_mod):
    tn = 1024
    n = N_MOD * D_MODEL
    return pl.pallas_call(
        _mod_kernel,
        grid=(DEPTH, n // tn),
        in_specs=[pl.BlockSpec((N_MOD_ROWS, D_MODEL), lambda l, j: (0, 0)),
                  pl.BlockSpec((None, D_MODEL, tn), lambda l, j: (l, 0, j)),
                  pl.BlockSpec((None, 1, tn), lambda l, j: (l, 0, j))],
        out_specs=pl.BlockSpec((None, N_MOD_ROWS, tn), lambda l, j: (l, 0, j)),
        out_shape=jax.ShapeDtypeStruct((DEPTH, N_MOD_ROWS, n), F32),
        compiler_params=_cparams(("parallel", "parallel")),
        name="modulation",
    )(c_rows, w_mod, b_mod.reshape(DEPTH, 1, n))


def _norm_mod(x, g, sc, sh):
    y = x * lax.rsqrt(jnp.mean(x * x, axis=-1, keepdims=True) + EPS) * g
    return y * (1.0 + sc) + sh


def _inproj_kernel(w_is_nk, x_ref, g_ref, sc_ref, sh_ref, w_ref, o_ref, hn_ref):
    @pl.when(pl.program_id(1) == 0)
    def _():
        hn_ref[...] = _norm_mod(x_ref[...], g_ref[...], sc_ref[...], sh_ref[...]).astype(BF16)

    w = w_ref[...].astype(BF16)
    o_ref[...] = _dot_nt(hn_ref[...], w) if w_is_nk else _dot(hn_ref[...], w)


def _inproj(x, g, sc, sh, w, name, w_is_nk=False):
    n = w.shape[0] if w_is_nk else w.shape[1]
    tm = PROJ_TM
    tn = min(n, 1024)
    mod_spec = pl.BlockSpec((None, 1, D_MODEL), lambda i, j: (_mod_row(i, tm), 0, 0))
    w_spec = (pl.BlockSpec((tn, D_MODEL), lambda i, j: (j, 0)) if w_is_nk
              else pl.BlockSpec((D_MODEL, tn), lambda i, j: (0, j)))
    return pl.pallas_call(
        functools.partial(_inproj_kernel, w_is_nk),
        grid=(T_ALL // tm, n // tn),
        in_specs=[pl.BlockSpec((tm, D_MODEL), lambda i, j: (i, 0)),
                  pl.BlockSpec((1, D_MODEL), lambda i, j: (0, 0)),
                  mod_spec, mod_spec, w_spec],
        out_specs=pl.BlockSpec((tm, tn), lambda i, j: (i, j)),
        out_shape=jax.ShapeDtypeStruct((T_ALL, n), F32),
        scratch_shapes=[pltpu.VMEM((tm, D_MODEL), BF16)],
        compiler_params=_cparams(("parallel", "arbitrary")),
        name=name,
    )(x, g.reshape(1, D_MODEL), sc, sh, w)


ODD_SEGS = (SSD_DIM, SSD_CONV_CH, 2 * SSD_HEADS, GLA_QK_DIM, GLA_QK_DIM, GLA_V_DIM, GLA_V_DIM, 2 * GLA_RANK)
ODD_SEG_START = tuple(int(v) for v in np.cumsum((0,) + ODD_SEGS))
ODD_IN = ODD_SEG_START[-1]
ODD_MAIN = ODD_IN - 2 * SSD_HEADS - 2 * GLA_RANK
RELAYOUT_ROWS = 1024
ODD_BLOCK_SRC = tuple(ODD_SEG_START[s] + r
                      for s in (0, 5, 6, 3, 4, 1) for r in range(0, ODD_SEGS[s], RELAYOUT_ROWS))


def _relayout_kernel(layer, src_ref, w_hbm, main_ref, small_ref, buf, sbuf, sem):
    j = pl.program_id(0)

    @pl.when(j == 0)
    def _():
        dt_rows, lr_rows = ODD_SEGS[2], ODD_SEGS[7]
        c_dt = pltpu.make_async_copy(w_hbm.at[layer, pl.ds(ODD_SEG_START[2], dt_rows)], sbuf.at[pl.ds(0, dt_rows)],
                                     sem.at[1])
        c_lr = pltpu.make_async_copy(w_hbm.at[layer, pl.ds(ODD_SEG_START[7], lr_rows)],
                                     sbuf.at[pl.ds(dt_rows, lr_rows)], sem.at[2])
        c_dt.start()
        c_lr.start()
        sbuf[pl.ds(dt_rows + lr_rows, SMALL_W - dt_rows - lr_rows), :] = jnp.zeros(
            (SMALL_W - dt_rows - lr_rows, D_MODEL), F32)
        c_dt.wait()
        c_lr.wait()
        small_ref[...] = sbuf[...].astype(BF16)

    cp = pltpu.make_async_copy(w_hbm.at[layer, pl.ds(pl.multiple_of(src_ref[j], 64), RELAYOUT_ROWS)], buf, sem.at[0])
    cp.start()
    cp.wait()
    main_ref[...] = buf[...].astype(BF16)


def _relayout_odd_weights(w_nk, layer):
    assert len(ODD_BLOCK_SRC) * RELAYOUT_ROWS == ODD_MAIN
    return pl.pallas_call(
        functools.partial(_relayout_kernel, layer),
        grid_spec=pltpu.PrefetchScalarGridSpec(
            num_scalar_prefetch=1,
            grid=(len(ODD_BLOCK_SRC),),
            in_specs=[_any_spec()],
            out_specs=[pl.BlockSpec((RELAYOUT_ROWS, D_MODEL), lambda j, src: (j, 0)),
                       pl.BlockSpec((SMALL_W, D_MODEL), lambda j, src: (0, 0))],
            scratch_shapes=[pltpu.VMEM((RELAYOUT_ROWS, D_MODEL), F32), pltpu.VMEM((SMALL_W, D_MODEL), F32),
                            pltpu.SemaphoreType.DMA((3,))]),
        out_shape=[jax.ShapeDtypeStruct((ODD_MAIN, D_MODEL), BF16), jax.ShapeDtypeStruct((SMALL_W, D_MODEL), BF16)],
        compiler_params=_cparams(("arbitrary",)),
        name="relayout_odd_weights",
    )(jnp.asarray(ODD_BLOCK_SRC, jnp.int32), w_nk)


def _outproj_kernel(a_ref, b_ref, w1_ref, w2_ref, x_ref, g_ref, o_ref):
    acc = _dot(a_ref[...], w1_ref[...].astype(BF16)) + _dot(b_ref[...], w2_ref[...].astype(BF16))
    o_ref[...] = x_ref[...] + g_ref[...] * acc


def _outproj(a, b, w, x, gate, name):
    k1 = a.shape[1]
    tm = PROJ_TM
    tn = 512
    return pl.pallas_call(
        _outproj_kernel,
        grid=(T_ALL // tm, D_MODEL // tn),
        in_specs=[pl.BlockSpec((tm, k1), lambda i, j: (i, 0)),
                  pl.BlockSpec((tm, k1), lambda i, j: (i, 0)),
                  pl.BlockSpec((k1, tn), lambda i, j: (0, j)),
                  pl.BlockSpec((k1, tn), lambda i, j: (1, j)),
                  pl.BlockSpec((tm, tn), lambda i, j: (i, j)),
                  pl.BlockSpec((None, 1, tn), lambda i, j: (_mod_row(i, tm), 0, j))],
        out_specs=pl.BlockSpec((tm, tn), lambda i, j: (i, j)),
        out_shape=jax.ShapeDtypeStruct((T_ALL, D_MODEL), F32),
        compiler_params=_cparams(("parallel", "parallel")),
        name=name,
    )(a, b, w, w, x, gate)


def _sgate_kernel(u_ref, v_ref, g_ref, ws_ref, bs_ref, o_ref):
    u = _gelu_tanh(u_ref[...])
    v = _gelu_tanh(v_ref[...])
    v = v * lax.rsqrt(jnp.mean(v * v, axis=-1, keepdims=True) + EPS) * g_ref[...]
    vb = v.astype(BF16)
    for c in range(TM // SG_CHUNK):
        rows = slice(c * SG_CHUNK, (c + 1) * SG_CHUNK)
        for g in range(SG_GROUPS):
            cols = slice(g * 128, (g + 1) * 128)
            s = _dot(ws_ref[g], vb[rows, cols]) + bs_ref[g]
            o_ref[rows, cols] = (u[rows, cols] * s).astype(o_ref.dtype)


def _spatial_gating(h, sg_norm, sg_w, sg_b):
    bias = jnp.broadcast_to(sg_b[:, :, None], (SG_GROUPS, SG_CHUNK, 128))
    return pl.pallas_call(
        _sgate_kernel,
        grid=(T_ALL // TM,),
        in_specs=[pl.BlockSpec((TM, SG_DIM), lambda i: (i, 0)),
                  pl.BlockSpec((TM, SG_DIM), lambda i: (i, 1)),
                  pl.BlockSpec((1, SG_DIM), lambda i: (0, 0)),
                  pl.BlockSpec((SG_GROUPS, SG_CHUNK, SG_CHUNK), lambda i: (0, 0, 0)),
                  pl.BlockSpec((SG_GROUPS, SG_CHUNK, 128), lambda i: (0, 0, 0))],
        out_specs=pl.BlockSpec((TM, SG_DIM), lambda i: (i, 0)),
        out_shape=jax.ShapeDtypeStruct((T_ALL, SG_DIM), BF16),
        compiler_params=_cparams(("parallel",)),
        name="spatial_gating",
    )(h, h, sg_norm.reshape(1, SG_DIM), sg_w.astype(BF16), bias)


def _softmax_rows(s):
    e = jnp.exp(s - jnp.max(s, axis=-1, keepdims=True))
    return e / jnp.sum(e, axis=-1, keepdims=True)


def _ctx_attn_kernel(*refs):
    q_ref, k_ref, v_ref = refs[:3]
    o_ref, kv_ref = refs[-2:]
    for hh in range(NA_HEADS):
        cs = slice(hh * NA_HEAD_DIM, (hh + 1) * NA_HEAD_DIM)
        k = k_ref[:, cs]
        v = v_ref[:, cs]
        s = _dot_nt(q_ref[:, cs].astype(BF16), k.astype(BF16)) * (NA_HEAD_DIM ** -0.5)
        p = _softmax_rows(s).astype(BF16)
        o_ref[:, cs] = _dot(p, v.astype(BF16)).astype(o_ref.dtype)
        kv_ref[0, hh] = k
        kv_ref[1, hh] = v


def _context_attention(h, j, kv_prev):
    qc = 2 * SG_DIM // NA_DIM
    in_specs = [pl.BlockSpec((L_CTX, NA_DIM), lambda b: (b, qc)),
                pl.BlockSpec((L_CTX, NA_DIM), lambda b: (b, qc + 1)),
                pl.BlockSpec((L_CTX, NA_DIM), lambda b: (b, qc + 2))]
    args = [h, h, h]
    aliases = {}
    if kv_prev is not None:
        in_specs.append(_any_spec())
        args.append(kv_prev)
        aliases = {3: 1}
    return pl.pallas_call(
        _ctx_attn_kernel,
        grid=(N_CTX_SEQ,),
        in_specs=in_specs,
        out_specs=[pl.BlockSpec((L_CTX, NA_DIM), lambda b: (b, 0)),
                   pl.BlockSpec((None, None, 2, NA_HEADS, L_CTX, NA_HEAD_DIM), lambda b: (b, j, 0, 0, 0, 0))],
        out_shape=[jax.ShapeDtypeStruct((T_ALL, NA_DIM), BF16),
                   jax.ShapeDtypeStruct((N_CTX_SEQ, N_EVEN, 2, NA_HEADS, L_CTX, NA_HEAD_DIM), F32)],
        input_output_aliases=aliases,
        compiler_params=_cparams(("parallel",)),
        name="context_attention",
    )(*args)


def _na_bias_table(rpb):
    col = np.arange(GRID_W)
    c0 = np.clip(col - WIN_C // 2, 0, GRID_W - WIN_C)
    col_ok = (col[None, :] >= c0[:, None]) & (col[None, :] < c0[:, None] + WIN_C)
    dc = np.clip(col[None, :] - col[:, None], -(WIN_C - 1), WIN_C - 1) + (WIN_C - 1)
    onehot = (dc.reshape(-1)[None, :] == np.arange(2 * WIN_C - 1)[:, None]).astype(np.float32)
    t = jnp.einsum("hrc,cn->hrn", rpb.astype(F32), onehot, precision=lax.Precision.HIGHEST)
    t = jnp.where(col_ok.reshape(-1), t, NEG_INF).reshape(NA_HEADS, 2 * WIN_R - 1, GRID_W, GRID_W)
    return jnp.stack([jnp.concatenate([t[:, d0 + w] for w in range(WIN_R)], axis=-1) for d0 in range(WIN_R)], axis=1)


def _na_attn_kernel(q_ref, k_ref, v_ref, kc_ref, vc_ref, bias_ref, o_in_ref, o_ref, kb, vb, kcb, vcb):
    del o_in_ref
    kb[...] = k_ref[...].astype(BF16)
    vb[...] = v_ref[...].astype(BF16)
    kcb[...] = kc_ref[...].astype(BF16)
    vcb[...] = vc_ref[...].astype(BF16)
    scale = NA_HEAD_DIM ** -0.5
    win = WIN_R * GRID_W

    def row(r, carry):
        r0 = jnp.clip(r - WIN_R // 2, 0, NA_ROWS - WIN_R)
        d0 = r0 - r + (WIN_R - 1)
        qs = pl.multiple_of(r * GRID_W, GRID_W)
        ks = pl.multiple_of(r0 * GRID_W, GRID_W)
        q = q_ref[pl.ds(qs, GRID_W), :].astype(BF16)
        s_loc = _dot_nt(q, kb[pl.ds(ks, win), :]) * scale + bias_ref[d0]
        s_ctx = _dot_nt(q, kcb[...]) * scale
        m = jnp.maximum(jnp.max(s_loc, axis=-1, keepdims=True), jnp.max(s_ctx, axis=-1, keepdims=True))
        e_loc = jnp.exp(s_loc - m)
        e_ctx = jnp.exp(s_ctx - m)
        den = jnp.sum(e_loc, axis=-1, keepdims=True) + jnp.sum(e_ctx, axis=-1, keepdims=True)
        o = _dot((e_loc / den).astype(BF16), vb[pl.ds(ks, win), :]) + _dot((e_ctx / den).astype(BF16), vcb[...])
        o_ref[pl.ds(qs, GRID_W), :] = o.astype(o_ref.dtype)
        return carry

    lax.fori_loop(0, NA_ROWS, row, 0, unroll=NA_UNROLL)


def _neighbourhood_attention(h, cache_kv, j, bias, o_prev):
    qc = 2 * SG_DIM // NA_HEAD_DIM
    rb = T_CTX // L_LAT
    cache_spec = lambda kv: pl.BlockSpec((None, None, None, None, PAST_LEN, NA_HEAD_DIM),
                                         lambda b, hh: (b, j, kv, hh, 0, 0))
    return pl.pallas_call(
        _na_attn_kernel,
        grid=(N_LAT_SEQ, NA_HEADS),
        in_specs=[pl.BlockSpec((L_LAT, NA_HEAD_DIM), lambda b, hh: (rb + b, qc + hh)),
                  pl.BlockSpec((L_LAT, NA_HEAD_DIM), lambda b, hh: (rb + b, qc + NA_HEADS + hh)),
                  pl.BlockSpec((L_LAT, NA_HEAD_DIM), lambda b, hh: (rb + b, qc + 2 * NA_HEADS + hh)),
                  cache_spec(0), cache_spec(1),
                  pl.BlockSpec((None, WIN_R, GRID_W, WIN_R * GRID_W), lambda b, hh: (hh, 0, 0, 0)),
                  _any_spec()],
        out_specs=pl.BlockSpec((L_LAT, NA_HEAD_DIM), lambda b, hh: (rb + b, hh)),
        out_shape=jax.ShapeDtypeStruct((T_ALL, NA_DIM), BF16),
        input_output_aliases={6: 0},
        scratch_shapes=[pltpu.VMEM((L_LAT, NA_HEAD_DIM), BF16), pltpu.VMEM((L_LAT, NA_HEAD_DIM), BF16),
                        pltpu.VMEM((PAST_LEN, NA_HEAD_DIM), BF16), pltpu.VMEM((PAST_LEN, NA_HEAD_DIM), BF16)],
        compiler_params=_cparams(("parallel", "parallel")),
        name="neighbourhood_attention",
    )(h, h, h, cache_kv, cache_kv, bias, o_prev)


CONV_ROWS = 256
CONV_COLS = 512


def _conv_kernel(transpose_out, x_ref, p_ref, n_ref, w_ref, b_ref, o_ref):
    i = pl.program_id(0)
    per = L_LAT // CONV_ROWS
    nct = T_CTX // CONV_ROWS
    pos = jnp.where(i < nct, 0, (i - nct) % per)
    last = jnp.where(i < nct, 0, per - 1)
    x = x_ref[...]
    prev_row = jnp.where(pos == 0, 0.0, p_ref[7:8, :])
    next_row = jnp.where(pos == last, 0.0, n_ref[0:1, :])
    ridx = lax.broadcasted_iota(jnp.int32, x.shape, 0)
    x_dn = jnp.where(ridx == 0, prev_row, pltpu.roll(x, 1, 0))
    x_up = jnp.where(ridx == CONV_ROWS - 1, next_row, pltpu.roll(x, CONV_ROWS - 1, 0))
    y = _silu(x_dn * w_ref[0:1, :] + x * w_ref[1:2, :] + x_up * w_ref[2:3, :] + b_ref[...])
    if transpose_out:
        for c in range(CONV_ROWS // SSD_CHUNK):
            o_ref[c] = y[c * SSD_CHUNK:(c + 1) * SSD_CHUNK, :].T
    else:
        o_ref[...] = y


def _conv_silu(h, conv_w, conv_b, col0, ncols, transpose_out):
    cb0 = (ODD_COL_XBC + col0) // CONV_COLS
    wb0 = col0 // CONV_COLS
    rb = CONV_ROWS // 8
    nblk8 = T_ALL // 8
    cpb = CONV_ROWS // SSD_CHUNK
    w8 = jnp.zeros((8, SSD_CONV_CH), F32).at[:3].set(conv_w)
    if transpose_out:
        out_spec = pl.BlockSpec((cpb, CONV_COLS, SSD_CHUNK), lambda i, j: (i, j, 0))
        out_shape = jax.ShapeDtypeStruct((T_ALL // SSD_CHUNK, ncols, SSD_CHUNK), F32)
    else:
        out_spec = pl.BlockSpec((CONV_ROWS, CONV_COLS), lambda i, j: (i, j))
        out_shape = jax.ShapeDtypeStruct((T_ALL, ncols), F32)
    return pl.pallas_call(
        functools.partial(_conv_kernel, transpose_out),
        grid=(T_ALL // CONV_ROWS, ncols // CONV_COLS),
        in_specs=[pl.BlockSpec((CONV_ROWS, CONV_COLS), lambda i, j: (i, cb0 + j)),
                  pl.BlockSpec((8, CONV_COLS), lambda i, j: (jnp.maximum(i * rb - 1, 0), cb0 + j)),
                  pl.BlockSpec((8, CONV_COLS), lambda i, j: (jnp.minimum((i + 1) * rb, nblk8 - 1), cb0 + j)),
                  pl.BlockSpec((8, CONV_COLS), lambda i, j: (0, wb0 + j)),
                  pl.BlockSpec((1, CONV_COLS), lambda i, j: (0, wb0 + j))],
        out_specs=out_spec,
        out_shape=out_shape,
        compiler_params=_cparams(("parallel", "parallel")),
        name="conv_silu_t" if transpose_out else "conv_silu",
    )(h, h, h, w8, conv_b.reshape(1, SSD_CONV_CH))


def _seq_info(t, chunk):
    nct = L_CTX // chunk
    ncl = L_LAT // chunk
    nctx = N_CTX_SEQ * nct
    is_ctx = t < nctx
    tl = jnp.maximum(t - nctx, 0)
    seq = jnp.where(is_ctx, t // nct, N_CTX_SEQ + tl // ncl)
    c = jnp.where(is_ctx, t % nct, tl % ncl)
    n = jnp.where(is_ctx, nct, ncl)
    first = jnp.where(is_ctx, (t // nct) * nct, nctx + (tl // ncl) * ncl)
    return seq, c, n, first


def _row_blk(t, chunk, rev):
    if not rev:
        return t
    _, c, n, first = _seq_info(t, chunk)
    return first + (n - 1 - c)


def _lat_seq(t, chunk):
    return jnp.maximum(_seq_info(t, chunk)[0] - N_CTX_SEQ, 0)


def _ctx_seq(t, chunk):
    return jnp.minimum(_seq_info(t, chunk)[0], N_CTX_SEQ - 1)


def _head_rows(mat, d, g):
    base = d * SSD_HEADS + g * SSD_GROUP_HEADS
    return jnp.concatenate(
        [jnp.broadcast_to(mat[base + r:base + r + 1, :], (SSD_HEAD_DIM, mat.shape[1])) for r in range(SSD_GROUP_HEADS)],
        axis=0)


def _ssd_kernel(rev, has_prev, *refs):
    refs = list(refs)
    xt_ref, bc_ref, sm_ref, dtb_ref, alog_ref, s0_ref = refs[:6]
    pos = 6
    if rev:
        yf_ref, z_ref, dcol_ref, gn_ref = refs[pos:pos + 4]
        pos += 4
    if has_prev:
        pos += 1
    y_ref, sfin_ref, s_scr = refs[pos:pos + 3]
    yt_scr = refs[pos + 3] if rev else None

    t = pl.program_id(0)
    seq, c, n, _ = _seq_info(t, SSD_CHUNK)
    is_ctx = seq < N_CTX_SEQ
    q = SSD_CHUNK
    d = 1 if rev else 0

    @pl.when(c == 0)
    def _():
        s_scr[...] = jnp.where(is_ctx, 0.0, s0_ref[...])

    ii = lax.broadcasted_iota(jnp.int32, (q, q), 0)
    jj = lax.broadcasted_iota(jnp.int32, (q, q), 1)
    mask = (jj >= ii) if rev else (jj <= ii)
    tri = mask.astype(BF16)

    dt = _softplus(sm_ref[...] + dtb_ref[...])
    dta = dt * (-jnp.exp(alog_ref[...]))
    cum = _mask_dot(tri, dta)
    tot = cum[0:1, :] if rev else cum[q - 1:q, :]
    cum_t = cum.T
    dt_t = dt.T
    te_t = (jnp.exp(tot - cum) * dt).T
    ecum_t = jnp.exp(cum_t)
    last = 0 if rev else q - 1
    etot_t = jnp.broadcast_to(ecum_t[:, last:last + 1], (q, q))

    for g in range(SSD_GROUPS):
        gs = slice(g * SSD_GROUP_DIM, (g + 1) * SSD_GROUP_DIM)
        bg = bc_ref[:, g * SSD_STATE:(g + 1) * SSD_STATE].astype(BF16)
        cg = bc_ref[:, SSD_BC + g * SSD_STATE:SSD_BC + (g + 1) * SSD_STATE].astype(BF16)
        cb = _dot_nt(cg, bg)
        s_g = s_scr[gs, :]
        x_g = xt_ref[gs, :]
        y_off = _dot_nt(s_g.astype(BF16), cg) * _head_rows(ecum_t, d, g)
        st = _dot((x_g * _head_rows(te_t, d, g)).astype(BF16), bg)
        s_scr[gs, :] = s_g * _head_rows(etot_t, d, g) + st
        for k in range(SSD_GROUP_HEADS // 2):
            w_pair = []
            for r in (2 * k, 2 * k + 1):
                col = d * SSD_HEADS + g * SSD_GROUP_HEADS + r
                seg = cum[:, col:col + 1] - cum_t[col:col + 1, :]
                w_pair.append((cb * jnp.where(mask, jnp.exp(seg), 0.0) * dt_t[col:col + 1, :]).astype(BF16))
            w2 = jnp.concatenate(w_pair, axis=1)
            ps = slice(k * 2 * SSD_HEAD_DIM, (k + 1) * 2 * SSD_HEAD_DIM)
            xp = x_g[ps, :].astype(BF16)
            rr = lax.broadcasted_iota(jnp.int32, xp.shape, 0)
            zero = jnp.zeros_like(xp)
            x2 = jnp.concatenate([jnp.where(rr < SSD_HEAD_DIM, xp, zero), jnp.where(rr >= SSD_HEAD_DIM, xp, zero)],
                                 axis=1)
            y_pair = _dot_nt(x2, w2) + y_off[ps, :]
            rows = slice(g * SSD_GROUP_DIM + k * 2 * SSD_HEAD_DIM, g * SSD_GROUP_DIM + (k + 1) * 2 * SSD_HEAD_DIM)
            if rev:
                yt_scr[rows, :] = y_pair
            else:
                y_ref[rows, :] = y_pair

    if rev:
        yt = yf_ref[...] + yt_scr[...] + dcol_ref[...] * xt_ref[...]
        y = yt.T * _silu(z_ref[...])
        y = y * lax.rsqrt(jnp.mean(y * y, axis=-1, keepdims=True) + EPS) * gn_ref[...]
        y_ref[...] = y.astype(y_ref.dtype)

    @pl.when((c == n - 1) & is_ctx)
    def _():
        sfin_ref[...] = s_scr[...]


def _ssd_scan(rev, layer, xt, bc, small, dtb, alog, s0, states_prev, extra=None):
    q = SSD_CHUNK
    d = 1 if rev else 0
    blk = lambda t: _row_blk(t, q, rev)
    in_specs = [pl.BlockSpec((None, SSD_DIM, q), lambda t: (blk(t), 0, 0)),
                pl.BlockSpec((q, 2 * SSD_BC), lambda t: (blk(t), 0)),
                pl.BlockSpec((q, SMALL_W), lambda t: (blk(t), 0)),
                pl.BlockSpec((1, SMALL_W), lambda t: (0, 0)),
                pl.BlockSpec((1, SMALL_W), lambda t: (0, 0)),
                pl.BlockSpec((None, SSD_DIM, SSD_STATE), lambda t: (_lat_seq(t, q), 0, 0))]
    args = [xt, bc, small, dtb, alog, s0]
    if rev:
        yf, h, dcol, gn = extra
        in_specs += [pl.BlockSpec((None, SSD_DIM, q), lambda t: (blk(t), 0, 0)),
                     pl.BlockSpec((q, SSD_DIM), lambda t: (blk(t), 0)),
                     pl.BlockSpec((SSD_DIM, q), lambda t: (0, 0)),
                     pl.BlockSpec((1, SSD_DIM), lambda t: (0, 0))]
        args += [yf, h, dcol, gn]
        y_spec = pl.BlockSpec((q, SSD_DIM), lambda t: (blk(t), 0))
        y_shape = jax.ShapeDtypeStruct((T_ALL, SSD_DIM), BF16)
    else:
        y_spec = pl.BlockSpec((None, SSD_DIM, q), lambda t: (blk(t), 0, 0))
        y_shape = jax.ShapeDtypeStruct((T_ALL // q, SSD_DIM, q), F32)
    aliases = {}
    if states_prev is not None:
        aliases = {len(args): 1}
        in_specs.append(_any_spec())
        args.append(states_prev)
    scratch = [pltpu.VMEM((SSD_DIM, SSD_STATE), F32)]
    if rev:
        scratch.append(pltpu.VMEM((SSD_DIM, q), F32))
    return pl.pallas_call(
        functools.partial(_ssd_kernel, rev, states_prev is not None),
        grid=(T_ALL // q,),
        in_specs=in_specs,
        out_specs=[y_spec,
                   pl.BlockSpec((None, None, None, SSD_DIM, SSD_STATE), lambda t: (_ctx_seq(t, q), layer, d, 0, 0))],
        out_shape=[y_shape, jax.ShapeDtypeStruct((N_CTX_SEQ, N_ODD, 2, SSD_DIM, SSD_STATE), F32)],
        input_output_aliases=aliases,
        scratch_shapes=scratch,
        compiler_params=_cparams(("arbitrary",)),
        name="ssd_bwd" if rev else "ssd_fwd",
    )(*args)


def _gla_kernel(rev, has_prev, *refs):
    refs = list(refs)
    q_ref, k_ref, v_ref, sm_ref, wa_ref, ba_ref, s0_ref = refs[:7]
    pos = 7
    if rev:
        of_ref, gate_ref, gn_ref = refs[pos:pos + 3]
        pos += 3
    if has_prev:
        pos += 1
    o_ref, sfin_ref, s_scr = refs[pos:pos + 3]

    t = pl.program_id(0)
    seq, c, n, _ = _seq_info(t, GLA_CHUNK)
    is_ctx = seq < N_CTX_SEQ
    cq = GLA_CHUNK

    @pl.when(c == 0)
    def _():
        for hh in range(GLA_HEADS):
            s_scr[hh] = jnp.where(is_ctx, 0.0, s0_ref[hh].T)

    ii = lax.broadcasted_iota(jnp.int32, (cq, cq), 0)
    jj = lax.broadcasted_iota(jnp.int32, (cq, cq), 1)
    mask = (jj >= ii) if rev else (jj <= ii)
    tri = mask.astype(BF16)

    g_lin = _dot(sm_ref[...].astype(BF16), wa_ref[...]) + ba_ref[...]
    g = -_softplus(-g_lin) / GLA_GATE_NORM
    cum = _mask_dot(tri, g)
    last = cum[0:1, :] if rev else cum[cq - 1:cq, :]
    q_e = (q_ref[...] * (GLA_HEAD_K ** -0.5)) * jnp.exp(cum)
    k = k_ref[...]
    k_e = k * jnp.exp(-cum)
    k_end = k * jnp.exp(last - cum)
    e_last = jnp.exp(last)

    for hh in range(GLA_HEADS):
        ks = slice(hh * GLA_HEAD_K, (hh + 1) * GLA_HEAD_K)
        vs = slice(hh * GLA_HEAD_V, (hh + 1) * GLA_HEAD_V)
        qh = q_e[:, ks].astype(BF16)
        vh = v_ref[:, vs]
        vh_b = vh.astype(BF16)
        s_prev = s_scr[hh]
        att = jnp.where(mask, _dot_nt(qh, k_e[:, ks].astype(BF16)), 0.0)
        o = _dot(att.astype(BF16), vh_b) + _dot_nt(qh, s_prev.astype(BF16))
        st = _dot(vh.T.astype(BF16), k_end[:, ks].astype(BF16))
        s_scr[hh] = s_prev * e_last[:, ks] + st
        if rev:
            o = o + of_ref[:, vs]
            o = o * lax.rsqrt(jnp.mean(o * o, axis=-1, keepdims=True) + EPS) * gn_ref[...]
            o_ref[:, vs] = (o * _silu(gate_ref[:, vs])).astype(o_ref.dtype)
        else:
            o_ref[:, vs] = o

    @pl.when((c == n - 1) & is_ctx)
    def _():
        for hh in range(GLA_HEADS):
            sfin_ref[hh] = s_scr[hh].T


def _gla_scan(rev, layer, h, small, wa, ba, s0, states_prev, extra=None):
    cq = GLA_CHUNK
    d = 1 if rev else 0
    cv = SSD_DIM // GLA_V_DIM
    cqk = (SSD_DIM + 2 * GLA_V_DIM) // GLA_QK_DIM
    row = lambda t: _row_blk(t, cq, rev)
    in_specs = [pl.BlockSpec((cq, GLA_QK_DIM), lambda t: (row(t), cqk)),
                pl.BlockSpec((cq, GLA_QK_DIM), lambda t: (row(t), cqk + 1)),
                pl.BlockSpec((cq, GLA_V_DIM), lambda t: (row(t), cv)),
                pl.BlockSpec((cq, SMALL_W), lambda t: (row(t), 0)),
                pl.BlockSpec((SMALL_W, GLA_QK_DIM), lambda t: (0, 0)),
                pl.BlockSpec((1, GLA_QK_DIM), lambda t: (0, 0)),
                pl.BlockSpec((None, GLA_HEADS, GLA_HEAD_K, GLA_HEAD_V), lambda t: (_lat_seq(t, cq), 0, 0, 0))]
    args = [h, h, h, small, wa, ba, s0]
    if rev:
        o_f, gn = extra
        in_specs += [pl.BlockSpec((cq, GLA_V_DIM), lambda t: (row(t), 0)),
                     pl.BlockSpec((cq, GLA_V_DIM), lambda t: (row(t), cv + 1)),
                     pl.BlockSpec((1, GLA_HEAD_V), lambda t: (0, 0))]
        args += [o_f, h, gn]
    aliases = {}
    if states_prev is not None:
        aliases = {len(args): 1}
        in_specs.append(_any_spec())
        args.append(states_prev)
    return pl.pallas_call(
        functools.partial(_gla_kernel, rev, states_prev is not None),
        grid=(T_ALL // cq,),
        in_specs=in_specs,
        out_specs=[pl.BlockSpec((cq, GLA_V_DIM), lambda t: (row(t), 0)),
                   pl.BlockSpec((None, None, None, GLA_HEADS, GLA_HEAD_K, GLA_HEAD_V),
                                lambda t: (_ctx_seq(t, cq), layer, d, 0, 0, 0))],
        out_shape=[jax.ShapeDtypeStruct((T_ALL, GLA_V_DIM), BF16 if rev else F32),
                   jax.ShapeDtypeStruct((N_CTX_SEQ, N_ODD, 2, GLA_HEADS, GLA_HEAD_K, GLA_HEAD_V), F32)],
        input_output_aliases=aliases,
        scratch_shapes=[pltpu.VMEM((GLA_HEADS, GLA_HEAD_V, GLA_HEAD_K), F32)],
        compiler_params=_cparams(("arbitrary",)),
        name="gla_bwd" if rev else "gla_fwd",
    )(*args)


ROUTER_ROWS = 32


def _first_max(vals):
    out = []
    for a, va in enumerate(vals):
        ok = None
        for b, vb in enumerate(vals):
            if a == b:
                continue
            t = (va > vb) if b < a else (va >= vb)
            ok = t if ok is None else (ok & t)
        out.append(ok)
    return out


def _router_kernel(x_ref, g_ref, sc_ref, sh_ref, whi_ref, wlo_ref, bsel_ref, tri_ref,
                   hn_ref, gates_ref, rank_ref, cnt_ref, carry):
    i = pl.program_id(0)

    @pl.when(i == 0)
    def _():
        carry[...] = jnp.zeros_like(carry)

    hn = _norm_mod(x_ref[...], g_ref[...], sc_ref[...], sh_ref[...])
    hn_ref[...] = hn
    hb = hn.astype(BF16)
    hl = (hn - hb.astype(F32)).astype(BF16)
    logits = _dot(hb, whi_ref[...]) + _dot(hb, wlo_ref[...]) + _dot(hl, whi_ref[...])
    lt = logits.T
    score = [jax.nn.sigmoid(lt[8 * m:8 * m + 8, :]) for m in range(EXPERTS_PER_GROUP)]
    sel = [score[m] + bsel_ref[8 * m:8 * m + 8, :] for m in range(EXPERTS_PER_GROUP)]

    hi1, lo1 = jnp.maximum(sel[0], sel[1]), jnp.minimum(sel[0], sel[1])
    hi2, lo2 = jnp.maximum(sel[2], sel[3]), jnp.minimum(sel[2], sel[3])
    grp = jnp.maximum(hi1, hi2) + jnp.maximum(jnp.minimum(hi1, hi2), jnp.maximum(lo1, lo2))
    best = _first_max([grp[g:g + 1, :] for g in range(N_EXPERT_GROUPS)])
    srow = lax.broadcasted_iota(jnp.int32, grp.shape, 0)
    gmask = jnp.zeros(grp.shape, jnp.bool_)
    for g in range(N_EXPERT_GROUPS):
        gmask = gmask | ((srow == g) & best[g])

    first = _first_max(sel)
    sel2 = [jnp.where(first[m], -jnp.inf, sel[m]) for m in range(EXPERTS_PER_GROUP)]
    second = _first_max(sel2)
    chosen = [(first[m] | second[m]) & gmask for m in range(EXPERTS_PER_GROUP)]
    wsum = sum(jnp.where(chosen[m], score[m], 0.0) for m in range(EXPERTS_PER_GROUP))
    wtot = jnp.sum(wsum, axis=0, keepdims=True)
    cm = jnp.concatenate([chosen[m].astype(F32) for m in range(EXPERTS_PER_GROUP)], axis=0)
    sc_all = jnp.concatenate(score, axis=0)
    gates_ref[...] = jnp.where(cm > 0, sc_all / wtot, 0.0)
    within = _dot(cm.astype(BF16), tri_ref[...])
    rank_ref[...] = jnp.where(cm > 0, carry[:, 0:1] + within, -1.0)
    carry[...] = carry[...] + jnp.sum(cm, axis=1, keepdims=True)
    cnt_ref[...] = carry[...]


def _router(x, g, sc, sh, w_router, b_router):
    slot_of_expert = np.array([(e % EXPERTS_PER_GROUP) * 8 + e // EXPERTS_PER_GROUP for e in range(N_EXPERTS)])
    w128 = jnp.zeros((D_MODEL, 128), F32).at[:, slot_of_expert].set(w_router)
    w_hi = w128.astype(BF16)
    w_lo = (w128 - w_hi.astype(F32)).astype(BF16)
    b32 = jnp.zeros((ROUTER_ROWS,), F32).at[slot_of_expert].set(b_router.astype(F32))
    bsel = jnp.broadcast_to(b32[:, None], (ROUTER_ROWS, TM))
    tri = jnp.asarray(np.triu(np.ones((TM, TM), np.float32), 1), BF16)
    mod_spec = pl.BlockSpec((None, 1, D_MODEL), lambda i: (_mod_row(i, TM), 0, 0))
    hn, gates, rank, cnt = pl.pallas_call(
        _router_kernel,
        grid=(T_ALL // TM,),
        in_specs=[pl.BlockSpec((TM, D_MODEL), lambda i: (i, 0)),
                  pl.BlockSpec((1, D_MODEL), lambda i: (0, 0)),
                  mod_spec, mod_spec,
                  pl.BlockSpec((D_MODEL, 128), lambda i: (0, 0)),
                  pl.BlockSpec((D_MODEL, 128), lambda i: (0, 0)),
                  pl.BlockSpec((ROUTER_ROWS, TM), lambda i: (0, 0)),
                  pl.BlockSpec((TM, TM), lambda i: (0, 0))],
        out_specs=[pl.BlockSpec((TM, D_MODEL), lambda i: (i, 0)),
                   pl.BlockSpec((ROUTER_ROWS, TM), lambda i: (0, i)),
                   pl.BlockSpec((ROUTER_ROWS, TM), lambda i: (0, i)),
                   pl.BlockSpec((ROUTER_ROWS, 128), lambda i: (0, 0))],
        out_shape=[jax.ShapeDtypeStruct((T_ALL, D_MODEL), F32),
                   jax.ShapeDtypeStruct((ROUTER_ROWS, T_ALL), F32),
                   jax.ShapeDtypeStruct((ROUTER_ROWS, T_ALL), F32),
                   jax.ShapeDtypeStruct((ROUTER_ROWS, 128), F32)],
        scratch_shapes=[pltpu.VMEM((ROUTER_ROWS, 128), F32)],
        compiler_params=_cparams(("arbitrary",)),
        name="router",
    )(x, g.reshape(1, D_MODEL), sc, sh, w_hi, w_lo, bsel, tri)
    return hn, gates[slot_of_expert], rank[slot_of_expert], cnt[slot_of_expert, 0]


def _dispatch_plan(gates, rank, cnt):
    cnt = cnt.astype(jnp.int32)
    padded = ((cnt + MOE_TM - 1) // MOE_TM) * MOE_TM
    end = jnp.cumsum(padded)
    start = end - padded
    chosen = rank >= 0
    slot = start[:, None] + rank.astype(jnp.int32)
    slot_a = jnp.min(jnp.where(chosen, slot, MOE_ROWS), axis=0)
    slot_b = jnp.max(jnp.where(chosen, slot, -1), axis=0)
    w_a = jnp.sum(jnp.where(chosen & (slot == slot_a[None]), gates, 0.0), axis=0)
    w_b = jnp.sum(jnp.where(chosen & (slot == slot_b[None]), gates, 0.0), axis=0)
    tile_row = jnp.arange(MOE_TILES, dtype=jnp.int32) * MOE_TM
    tile_expert = jnp.minimum(jnp.sum(tile_row[:, None] >= end[None, :], axis=1), N_EXPERTS - 1).astype(jnp.int32)
    changed = jnp.concatenate([jnp.ones((1,), jnp.bool_), tile_expert[1:] != tile_expert[:-1]])
    tile_state = jnp.where(tile_row < end[-1], 1 + changed.astype(jnp.int32), 0).astype(jnp.int32)
    tidx = jnp.arange(MOE_TILES, dtype=jnp.int32)
    later_first = (tidx[None, :] > tidx[:, None]) & (tile_state[None, :] == 2)
    nxt = jnp.min(jnp.where(later_first, tidx[None, :], MOE_TILES), axis=1)
    next_expert = jnp.where(nxt < MOE_TILES, tile_expert[jnp.minimum(nxt, MOE_TILES - 1)], -1).astype(jnp.int32)
    return slot_a, slot_b, w_a, w_b, tile_expert, tile_state, next_expert


def _dispatch_kernel(sa_ref, sb_ref, hn_ref, xs_in_ref, xs_ref, sem):
    del xs_in_ref
    base = pl.program_id(0) * TM

    def row_copies(r):
        src = hn_ref.at[pl.ds(r, 1)]
        return (pltpu.make_async_copy(src, xs_ref.at[pl.ds(sa_ref[base + r], 1)], sem),
                pltpu.make_async_copy(src, xs_ref.at[pl.ds(sb_ref[base + r], 1)], sem))

    def start(r, carry):
        for cp in row_copies(r):
            cp.start()
        return carry

    def wait(r, carry):
        for cp in row_copies(r):
            cp.wait()
        return carry

    lax.fori_loop(0, TM, start, 0, unroll=DISPATCH_UNROLL)
    lax.fori_loop(0, TM, wait, 0, unroll=DISPATCH_UNROLL)


def _dispatch(hn, slot_a, slot_b, xs0):
    return pl.pallas_call(
        _dispatch_kernel,
        grid_spec=pltpu.PrefetchScalarGridSpec(
            num_scalar_prefetch=2,
            grid=(T_ALL // TM,),
            in_specs=[pl.BlockSpec((TM, D_MODEL), lambda i, sa, sb: (i, 0)), _any_spec()],
            out_specs=_any_spec(),
            scratch_shapes=[pltpu.SemaphoreType.DMA(())]),
        out_shape=jax.ShapeDtypeStruct((MOE_ROWS, D_MODEL), F32),
        input_output_aliases={3: 0},
        compiler_params=_cparams(("arbitrary",)),
        name="moe_dispatch",
    )(slot_a, slot_b, hn, xs0)


def _moe_kernel(layer, te_ref, ts_ref, nx_ref, x_ref, wg_hbm, wu_hbm, wd_hbm, o_ref,
                wg_f, wu_f, wd_f, wg_b, wu_b, wd_b, sem):
    i = pl.program_id(0)
    state = ts_ref[i]

    def weight_copies(e):
        return (pltpu.make_async_copy(wg_hbm.at[layer, e], wg_f, sem.at[0]),
                pltpu.make_async_copy(wu_hbm.at[layer, e], wu_f, sem.at[1]),
                pltpu.make_async_copy(wd_hbm.at[layer, e], wd_f, sem.at[2]))

    @pl.when(i == 0)
    def _():
        for cp in weight_copies(te_ref[0]):
            cp.start()

    @pl.when(state == 2)
    def _():
        cg, cu, cd = weight_copies(te_ref[i])
        cg.wait()
        wg_b[...] = wg_f[...].astype(BF16)
        cu.wait()
        wu_b[...] = wu_f[...].astype(BF16)
        cd.wait()
        wd_b[...] = wd_f[...].astype(BF16)

        @pl.when(nx_ref[i] >= 0)
        def _():
            for cp in weight_copies(nx_ref[i]):
                cp.start()

    @pl.when(state > 0)
    def _():
        x = x_ref[...].astype(BF16)
        act = (_silu(_dot(x, wg_b[...])) * _dot(x, wu_b[...])).astype(BF16)
        o_ref[...] = _dot(act, wd_b[...])

    @pl.when(state == 0)
    def _():
        o_ref[...] = jnp.zeros_like(o_ref)


def _moe_experts(layer, tile_expert, tile_state, next_expert, xs, w_gate, w_up, w_down):
    return pl.pallas_call(
        functools.partial(_moe_kernel, layer),
        grid_spec=pltpu.PrefetchScalarGridSpec(
            num_scalar_prefetch=3,
            grid=(MOE_TILES,),
            in_specs=[pl.BlockSpec((MOE_TM, D_MODEL), lambda i, te, ts, nx: (i, 0)),
                      _any_spec(), _any_spec(), _any_spec()],
            out_specs=pl.BlockSpec((MOE_TM, D_MODEL), lambda i, te, ts, nx: (i, 0)),
            scratch_shapes=[pltpu.VMEM((D_MODEL, D_FF), F32), pltpu.VMEM((D_MODEL, D_FF), F32),
                            pltpu.VMEM((D_FF, D_MODEL), F32),
                            pltpu.VMEM((D_MODEL, D_FF), BF16), pltpu.VMEM((D_MODEL, D_FF), BF16),
                            pltpu.VMEM((D_FF, D_MODEL), BF16),
                            pltpu.SemaphoreType.DMA((3,))]),
        out_shape=jax.ShapeDtypeStruct((MOE_ROWS, D_MODEL), F32),
        compiler_params=_cparams(("arbitrary",)),
        name="moe_experts",
    )(tile_expert, tile_state, next_expert, xs, w_gate, w_up, w_down)


def _combine_kernel(x_ref, g_ref, ya_ref, yb_ref, wa_ref, wb_ref, o_ref):
    o_ref[...] = x_ref[...] + g_ref[...] * (wa_ref[...] * ya_ref[...] + wb_ref[...] * yb_ref[...])


def _combine(x, gate, ya, yb, w_a, w_b):
    row = pl.BlockSpec((TM, D_MODEL), lambda i: (i, 0))
    wspec = pl.BlockSpec((TM, 1), lambda i: (i, 0))
    return pl.pallas_call(
        _combine_kernel,
        grid=(T_ALL // TM,),
        in_specs=[row, pl.BlockSpec((None, 1, D_MODEL), lambda i: (_mod_row(i, TM), 0, 0)), row, row, wspec, wspec],
        out_specs=row,
        out_shape=jax.ShapeDtypeStruct((T_ALL, D_MODEL), F32),
        compiler_params=_cparams(("parallel",)),
        name="moe_combine",
    )(x, gate, ya, yb, w_a.reshape(T_ALL, 1), w_b.reshape(T_ALL, 1))


def _take_rows(a, idx):
    return a.at[idx].get(mode="promise_in_bounds")


def _moe_layer(layer, x, xs_buf, g, sc, sh, gate, w_router, b_router, w_gate, w_up, w_down):
    hn, gates, rank, cnt = _router(x, g, sc, sh, w_router, b_router)
    slot_a, slot_b, w_a, w_b, tile_expert, tile_state, next_expert = _dispatch_plan(gates, rank, cnt)
    xs = _dispatch(hn, slot_a, slot_b, xs_buf)
    y = _moe_experts(layer, tile_expert, tile_state, next_expert, xs, w_gate, w_up, w_down)
    return _combine(x, gate, _take_rows(y, slot_a), _take_rows(y, slot_b), w_a, w_b), xs


def _final_norm_kernel(x_ref, g_ref, oc_ref, ol_ref):
    x = x_ref[...]
    y = x * lax.rsqrt(jnp.mean(x * x, axis=-1, keepdims=True) + EPS) * g_ref[...]
    is_ctx = pl.program_id(0) < T_CTX // TM

    @pl.when(is_ctx)
    def _():
        oc_ref[...] = y

    @pl.when(jnp.logical_not(is_ctx))
    def _():
        ol_ref[...] = y


def _final_norm(x, g):
    nct = T_CTX // TM
    return pl.pallas_call(
        _final_norm_kernel,
        grid=(T_ALL // TM,),
        in_specs=[pl.BlockSpec((TM, D_MODEL), lambda i: (i, 0)), pl.BlockSpec((1, D_MODEL), lambda i: (0, 0))],
        out_specs=[pl.BlockSpec((TM, D_MODEL), lambda i: (jnp.minimum(i, nct - 1), 0)),
                   pl.BlockSpec((TM, D_MODEL), lambda i: (jnp.maximum(i - nct, 0), 0))],
        out_shape=[jax.ShapeDtypeStruct((T_CTX, D_MODEL), F32), jax.ShapeDtypeStruct((T_LAT, D_MODEL), F32)],
        compiler_params=_cparams(("arbitrary",)),
        name="final_norm",
    )(x, g.reshape(1, D_MODEL))


def _even_layer(x, mods, j, cache_kv, kv_new, norm_g, w_in, w_out, sg_norm, sg_w, sg_b, rpb):
    sh1, sc1, g1 = mods
    h = _inproj(x, norm_g, sc1, sh1, w_in, "inproj_even")
    a_out = _spatial_gating(h, sg_norm, sg_w, sg_b)
    o, kv_new = _context_attention(h, j, kv_new)
    o = _neighbourhood_attention(h, cache_kv, j, _na_bias_table(rpb), o)
    return _outproj(a_out, o, w_out, x, g1, "outproj_even"), kv_new


def _odd_layer(x, mods, j, state_ssm, state_gla, ssm_new, gla_new, norm_g, w_in, w_out, conv_w, conv_b,
               dt_bias, a_log, ssd_d, ssd_norm, w_a2, b_a, gla_norm):
    sh1, sc1, g1 = mods
    w_main, w_small = _relayout_odd_weights(w_in, j)
    h = _inproj(x, norm_g, sc1, sh1, w_main, "inproj_odd", w_is_nk=True)
    small = _inproj(x, norm_g, sc1, sh1, w_small, "inproj_odd_small", w_is_nk=True)

    xt = _conv_silu(h, conv_w, conv_b, 0, SSD_DIM, True)
    bc = _conv_silu(h, conv_w, conv_b, SSD_DIM, 2 * SSD_BC, False)
    pad = jnp.zeros((SMALL_W - 2 * SSD_HEADS,), F32)
    dtb = jnp.concatenate([dt_bias.reshape(-1).astype(F32), pad]).reshape(1, SMALL_W)
    alog = jnp.concatenate([a_log.reshape(-1).astype(F32), pad]).reshape(1, SMALL_W)
    s0 = state_ssm.reshape(N_LAT_SEQ, 2, SSD_DIM, SSD_STATE)
    y_f, ssm_new = _ssd_scan(False, j, xt, bc, small, dtb, alog, s0[:, 0], ssm_new)
    dcol = jnp.broadcast_to(jnp.repeat(ssd_d.astype(F32), SSD_HEAD_DIM)[:, None], (SSD_DIM, SSD_CHUNK))
    y_c, ssm_new = _ssd_scan(True, j, xt, bc, small, dtb, alog, s0[:, 1], ssm_new,
                             (y_f, h, dcol, ssd_norm.reshape(1, SSD_DIM)))

    def wa_pad(dd):
        lo = 2 * SSD_HEADS + dd * GLA_RANK
        return jnp.zeros((SMALL_W, GLA_QK_DIM), F32).at[lo:lo + GLA_RANK].set(w_a2[dd]).astype(BF16)

    o_f, gla_new = _gla_scan(False, j, h, small, wa_pad(0), b_a[0].reshape(1, GLA_QK_DIM), state_gla[:, 0], gla_new)
    y_d, gla_new = _gla_scan(True, j, h, small, wa_pad(1), b_a[1].reshape(1, GLA_QK_DIM), state_gla[:, 1], gla_new,
                             (o_f, gla_norm.reshape(1, GLA_HEAD_V)))

    return _outproj(y_c, y_d, w_out, x, g1, "outproj_odd"), ssm_new, gla_new


def kernel(x_prompt, x_sample, cache_kv, state_ssm, state_gla, c, c_ctx, w_mod, b_mod, norm_mix, norm_ffn,
           norm_final, w_in_even, w_out_even, sg_norm, sg_w, sg_b, na_rpb, w_in_odd, w_out_odd, ssd_conv_w,
           ssd_conv_b, ssd_dt_bias, ssd_a_log, ssd_d, ssd_norm, gla_w_a2, gla_b_a, gla_norm, w_router, b_router,
           w_gate, w_up, w_down):
    x = jnp.concatenate([x_prompt.reshape(T_CTX, D_MODEL), x_sample.reshape(T_LAT, D_MODEL)], axis=0)
    c_rows = jnp.concatenate([c_ctx[None, :], c, jnp.zeros((N_MOD_ROWS - 1 - N_LAT_SEQ, D_MODEL), F32)], axis=0)
    mod = _modulation(c_rows, w_mod, b_mod)

    kv_new, ssm_new, gla_new = None, None, None
    xs_buf = jnp.zeros((MOE_ROWS, D_MODEL), F32)
    w_in_odd_nk = jnp.transpose(w_in_odd, (0, 2, 1))
    for i in range(DEPTH):
        m = mod[i, :1 + N_LAT_SEQ].reshape(1 + N_LAT_SEQ, N_MOD, 1, D_MODEL)
        sh1, sc1, g1, sh2, sc2, g2 = [m[:, s] for s in range(N_MOD)]
        j = i // 2
        if i % 2 == 0:
            x, kv_new = _even_layer(x, (sh1, sc1, g1), j, cache_kv, kv_new, norm_mix[i], w_in_even[j],
                                    w_out_even[j], sg_norm[j], sg_w[j], sg_b[j], na_rpb[j])
        else:
            x, ssm_new, gla_new = _odd_layer(x, (sh1, sc1, g1), j, state_ssm[:, j], state_gla[:, j], ssm_new,
                                             gla_new, norm_mix[i], w_in_odd_nk, w_out_odd[j], ssd_conv_w[j],
                                             ssd_conv_b[j], ssd_dt_bias[j], ssd_a_log[j], ssd_d[j], ssd_norm[j],
                                             gla_w_a2[j], gla_b_a[j], gla_norm[j])
        x, xs_buf = _moe_layer(i, x, xs_buf, norm_ffn[i], sc2, sh2, g2, w_router, b_router, w_gate, w_up, w_down)

    y_ctx, y_lat = _final_norm(x, norm_final)
    return (y_ctx.reshape(N_CTX_SEQ, L_CTX, D_MODEL),
            y_lat.reshape(N_LAT_SEQ, L_LAT, D_MODEL),
            kv_new,
            ssm_new.reshape(N_CTX_SEQ, N_ODD, 2, SSD_HEADS, SSD_HEAD_DIM, SSD_STATE),
            gla_new)
```

```python
import functools

import numpy as np
import jax
import jax.numpy as jnp
from jax import lax
from jax.experimental import pallas as pl
from jax.experimental.pallas import tpu as pltpu

F32 = jnp.float32
BF16 = jnp.bfloat16

D_MODEL = 2048
DEPTH = 4
N_EVEN = 2
N_ODD = 2
N_MOD = 6
EPS = 1e-6
NEG_INF = -1e30

N_CTX_SEQ = 16
L_CTX = 256
N_LAT_SEQ = 2
L_LAT = 2048
PAST_LEN = 512
T_CTX = N_CTX_SEQ * L_CTX
T_LAT = N_LAT_SEQ * L_LAT
T_ALL = T_CTX + T_LAT
N_MOD_ROWS = 8

GRID_W = 64
SG_DIM = 1024
SG_GROUPS = 8
SG_CHUNK = 128
NA_DIM = 1024
NA_HEADS = 8
NA_HEAD_DIM = 128
WIN_R = 8
WIN_C = 16
NA_ROWS = L_LAT // GRID_W
NA_UNROLL = 4

SSD_DIM = 2048
SSD_HEADS = 32
SSD_HEAD_DIM = 64
SSD_GROUPS = 4
SSD_GROUP_HEADS = SSD_HEADS // SSD_GROUPS
SSD_GROUP_DIM = SSD_GROUP_HEADS * SSD_HEAD_DIM
SSD_STATE = 128
SSD_CHUNK = 128
SSD_BC = SSD_GROUPS * SSD_STATE
SSD_CONV_CH = SSD_DIM + 2 * SSD_BC
GLA_HEADS = 4
GLA_QK_DIM = 1024
GLA_V_DIM = 2048
GLA_HEAD_K = 256
GLA_HEAD_V = 512
GLA_RANK = 16
GLA_GATE_NORM = 16.0
GLA_CHUNK = 64
GLA_STEP = 2 * GLA_CHUNK
SMALL_W = 128
ODD_COL_XBC = SSD_DIM + 2 * GLA_V_DIM + 2 * GLA_QK_DIM

N_EXPERTS = 16
N_EXPERT_GROUPS = 4
EXPERTS_PER_GROUP = 4
D_FF = 1024
MOE_TM = 256
MOE_TILES = (2 * T_ALL) // MOE_TM + N_EXPERTS
MOE_ROWS = MOE_TILES * MOE_TM

DISPATCH_UNROLL = 8
TM = 512
PROJ_TM = 1024
VMEM_LIMIT = 56 * 1024 * 1024


def _cparams(sem):
    return pltpu.CompilerParams(dimension_semantics=sem, vmem_limit_bytes=VMEM_LIMIT)


def _mod_row(i, tm):
    nct = T_CTX // tm
    per = L_LAT // tm
    return jnp.where(i < nct, 0, 1 + (i - nct) // per)


def _silu(x):
    return x * jax.nn.sigmoid(x)


def _softplus(x):
    return jnp.maximum(x, 0.0) + jnp.log1p(jnp.exp(-jnp.abs(x)))


def _gelu_tanh(x):
    return 0.5 * x * (1.0 + jnp.tanh(np.sqrt(2.0 / np.pi).astype(np.float32) * (x + 0.044715 * (x * x * x))))


def _split3(x):
    hi = x.astype(BF16)
    r1 = x - hi.astype(F32)
    mid = r1.astype(BF16)
    lo = (r1 - mid.astype(F32)).astype(BF16)
    return hi, mid, lo


def _dot(a, b):
    return jnp.dot(a, b, preferred_element_type=F32)


def _dot_nt(a, b):
    return lax.dot_general(a, b, (((1,), (1,)), ((), ())), preferred_element_type=F32)


def _mask_dot(mask_bf16, x):
    hi, mid, lo = _split3(x)
    return _dot(mask_bf16, hi) + _dot(mask_bf16, mid) + _dot(mask_bf16, lo)


def _any_spec():
    return pl.BlockSpec(memory_space=pl.ANY)


def _mod_kernel(c_ref, w_ref, b_ref, o_ref):
    s = _silu(c_ref[...]).astype(BF16)
    o_ref[...] = _dot(s, w_ref[...].astype(BF16)) + b_ref[...]


def _modulation(c_rows, w_mod, b_mod):
    tn = 1024
    n = N_MOD * D_MODEL
    return pl.pallas_call(
        _mod_kernel,
        grid=(DEPTH, n // tn),
        in_specs=[pl.BlockSpec((N_MOD_ROWS, D_MODEL), lambda l, j: (0, 0)),
                  pl.BlockSpec((None, D_MODEL, tn), lambda l, j: (l, 0, j)),
                  pl.BlockSpec((None, 1, tn), lambda l, j: (l, 0, j))],
        out_specs=pl.BlockSpec((None, N_MOD_ROWS, tn), lambda l, j: (l, 0, j)),
        out_shape=jax.ShapeDtypeStruct((DEPTH, N_MOD_ROWS, n), F32),
        compiler_params=_cparams(("parallel", "parallel")),
        name="modulation",
    )(c_rows, w_mod, b_mod.reshape(DEPTH, 1, n))


def _norm_mod(x, g, sc, sh):
    y = x * lax.rsqrt(jnp.mean(x * x, axis=-1, keepdims=True) + EPS) * g
    return y * (1.0 + sc) + sh


def _inproj_kernel(w_is_nk, x_ref, g_ref, sc_ref, sh_ref, w_ref, o_ref, hn_ref):
    @pl.when(pl.program_id(1) == 0)
    def _():
        hn_ref[...] = _norm_mod(x_ref[...], g_ref[...], sc_ref[...], sh_ref[...]).astype(BF16)

    w = w_ref[...].astype(BF16)
    o_ref[...] = _dot_nt(hn_ref[...], w) if w_is_nk else _dot(hn_ref[...], w)


def _inproj(x, g, sc, sh, w, name, w_is_nk=False):
    n = w.shape[0] if w_is_nk else w.shape[1]
    tm = PROJ_TM
    tn = min(n, 1024)
    mod_spec = pl.BlockSpec((None, 1, D_MODEL), lambda i, j: (_mod_row(i, tm), 0, 0))
    w_spec = (pl.BlockSpec((tn, D_MODEL), lambda i, j: (j, 0)) if w_is_nk
              else pl.BlockSpec((D_MODEL, tn), lambda i, j: (0, j)))
    return pl.pallas_call(
        functools.partial(_inproj_kernel, w_is_nk),
        grid=(T_ALL // tm, n // tn),
        in_specs=[pl.BlockSpec((tm, D_MODEL), lambda i, j: (i, 0)),
                  pl.BlockSpec((1, D_MODEL), lambda i, j: (0, 0)),
                  mod_spec, mod_spec, w_spec],
        out_specs=pl.BlockSpec((tm, tn), lambda i, j: (i, j)),
        out_shape=jax.ShapeDtypeStruct((T_ALL, n), F32),
        scratch_shapes=[pltpu.VMEM((tm, D_MODEL), BF16)],
        compiler_params=_cparams(("parallel", "arbitrary")),
        name=name,
    )(x, g.reshape(1, D_MODEL), sc, sh, w)


ODD_SEGS = (SSD_DIM, SSD_CONV_CH, 2 * SSD_HEADS, GLA_QK_DIM, GLA_QK_DIM, GLA_V_DIM, GLA_V_DIM, 2 * GLA_RANK)
ODD_SEG_START = tuple(int(v) for v in np.cumsum((0,) + ODD_SEGS))
ODD_IN = ODD_SEG_START[-1]
ODD_MAIN = ODD_IN - 2 * SSD_HEADS - 2 * GLA_RANK
RELAYOUT_ROWS = 1024
ODD_BLOCK_SRC = tuple(ODD_SEG_START[s] + r
                      for s in (0, 5, 6, 3, 4, 1) for r in range(0, ODD_SEGS[s], RELAYOUT_ROWS))


def _relayout_kernel(layer, src_ref, w_hbm, main_ref, small_ref, buf, sbuf, sem):
    j = pl.program_id(0)
    slot = j % 2

    def block_copy(jb, s):
        src = w_hbm.at[layer, pl.ds(pl.multiple_of(src_ref[jb], 64), RELAYOUT_ROWS)]
        return pltpu.make_async_copy(src, buf.at[s], sem.at[s])

    @pl.when(j == 0)
    def _():
        block_copy(0, 0).start()
        dt_rows, lr_rows = ODD_SEGS[2], ODD_SEGS[7]
        c_dt = pltpu.make_async_copy(w_hbm.at[layer, pl.ds(ODD_SEG_START[2], dt_rows)], sbuf.at[pl.ds(0, dt_rows)],
                                     sem.at[2])
        c_lr = pltpu.make_async_copy(w_hbm.at[layer, pl.ds(ODD_SEG_START[7], lr_rows)],
                                     sbuf.at[pl.ds(dt_rows, lr_rows)], sem.at[3])
        c_dt.start()
        c_lr.start()
        sbuf[pl.ds(dt_rows + lr_rows, SMALL_W - dt_rows - lr_rows), :] = jnp.zeros(
            (SMALL_W - dt_rows - lr_rows, D_MODEL), F32)
        c_dt.wait()
        c_lr.wait()
        small_ref[...] = sbuf[...].astype(BF16)

    @pl.when(j + 1 < pl.num_programs(0))
    def _():
        block_copy(j + 1, 1 - slot).start()

    block_copy(j, slot).wait()
    main_ref[...] = buf[slot].astype(BF16)


def _relayout_odd_weights(w_nk, layer):
    assert len(ODD_BLOCK_SRC) * RELAYOUT_ROWS == ODD_MAIN
    return pl.pallas_call(
        functools.partial(_relayout_kernel, layer),
        grid_spec=pltpu.PrefetchScalarGridSpec(
            num_scalar_prefetch=1,
            grid=(len(ODD_BLOCK_SRC),),
            in_specs=[_any_spec()],
            out_specs=[pl.BlockSpec((RELAYOUT_ROWS, D_MODEL), lambda j, src: (j, 0)),
                       pl.BlockSpec((SMALL_W, D_MODEL), lambda j, src: (0, 0))],
            scratch_shapes=[pltpu.VMEM((2, RELAYOUT_ROWS, D_MODEL), F32), pltpu.VMEM((SMALL_W, D_MODEL), F32),
                            pltpu.SemaphoreType.DMA((4,))]),
        out_shape=[jax.ShapeDtypeStruct((ODD_MAIN, D_MODEL), BF16), jax.ShapeDtypeStruct((SMALL_W, D_MODEL), BF16)],
        compiler_params=_cparams(("arbitrary",)),
        name="relayout_odd_weights",
    )(jnp.asarray(ODD_BLOCK_SRC, jnp.int32), w_nk)


def _outproj_kernel(a_ref, b_ref, w1_ref, w2_ref, x_ref, g_ref, o_ref):
    acc = _dot(a_ref[...], w1_ref[...].astype(BF16)) + _dot(b_ref[...], w2_ref[...].astype(BF16))
    o_ref[...] = x_ref[...] + g_ref[...] * acc


def _outproj(a, b, w, x, gate, name):
    k1 = a.shape[1]
    tm = PROJ_TM
    tn = 512
    return pl.pallas_call(
        _outproj_kernel,
        grid=(T_ALL // tm, D_MODEL // tn),
        in_specs=[pl.BlockSpec((tm, k1), lambda i, j: (i, 0)),
                  pl.BlockSpec((tm, k1), lambda i, j: (i, 0)),
                  pl.BlockSpec((k1, tn), lambda i, j: (0, j)),
                  pl.BlockSpec((k1, tn), lambda i, j: (1, j)),
                  pl.BlockSpec((tm, tn), lambda i, j: (i, j)),
                  pl.BlockSpec((None, 1, tn), lambda i, j: (_mod_row(i, tm), 0, j))],
        out_specs=pl.BlockSpec((tm, tn), lambda i, j: (i, j)),
        out_shape=jax.ShapeDtypeStruct((T_ALL, D_MODEL), F32),
        compiler_params=_cparams(("parallel", "parallel")),
        name=name,
    )(a, b, w, w, x, gate)


def _sgate_kernel(u_ref, v_ref, g_ref, ws_ref, bs_ref, o_ref):
    u = _gelu_tanh(u_ref[...])
    v = _gelu_tanh(v_ref[...])
    v = v * lax.rsqrt(jnp.mean(v * v, axis=-1, keepdims=True) + EPS) * g_ref[...]
    vb = v.astype(BF16)
    for c in range(TM // SG_CHUNK):
        rows = slice(c * SG_CHUNK, (c + 1) * SG_CHUNK)
        for g in range(SG_GROUPS):
            cols = slice(g * 128, (g + 1) * 128)
            s = _dot(ws_ref[g], vb[rows, cols]) + bs_ref[g]
            o_ref[rows, cols] = (u[rows, cols] * s).astype(o_ref.dtype)


def _spatial_gating(h, sg_norm, sg_w, sg_b):
    bias = jnp.broadcast_to(sg_b[:, :, None], (SG_GROUPS, SG_CHUNK, 128))
    return pl.pallas_call(
        _sgate_kernel,
        grid=(T_ALL // TM,),
        in_specs=[pl.BlockSpec((TM, SG_DIM), lambda i: (i, 0)),
                  pl.BlockSpec((TM, SG_DIM), lambda i: (i, 1)),
                  pl.BlockSpec((1, SG_DIM), lambda i: (0, 0)),
                  pl.BlockSpec((SG_GROUPS, SG_CHUNK, SG_CHUNK), lambda i: (0, 0, 0)),
                  pl.BlockSpec((SG_GROUPS, SG_CHUNK, 128), lambda i: (0, 0, 0))],
        out_specs=pl.BlockSpec((TM, SG_DIM), lambda i: (i, 0)),
        out_shape=jax.ShapeDtypeStruct((T_ALL, SG_DIM), BF16),
        compiler_params=_cparams(("parallel",)),
        name="spatial_gating",
    )(h, h, sg_norm.reshape(1, SG_DIM), sg_w.astype(BF16), bias)


def _softmax_rows(s):
    e = jnp.exp(s - jnp.max(s, axis=-1, keepdims=True))
    return e / jnp.sum(e, axis=-1, keepdims=True)


def _ctx_attn_kernel(*refs):
    q_ref, k_ref, v_ref = refs[:3]
    o_ref, kv_ref = refs[-2:]
    for hh in range(NA_HEADS):
        cs = slice(hh * NA_HEAD_DIM, (hh + 1) * NA_HEAD_DIM)
        k = k_ref[:, cs]
        v = v_ref[:, cs]
        s = _dot_nt(q_ref[:, cs].astype(BF16), k.astype(BF16)) * (NA_HEAD_DIM ** -0.5)
        p = _softmax_rows(s).astype(BF16)
        o_ref[:, cs] = _dot(p, v.astype(BF16)).astype(o_ref.dtype)
        kv_ref[0, hh] = k
        kv_ref[1, hh] = v


def _context_attention(h, j, kv_prev):
    qc = 2 * SG_DIM // NA_DIM
    in_specs = [pl.BlockSpec((L_CTX, NA_DIM), lambda b: (b, qc)),
                pl.BlockSpec((L_CTX, NA_DIM), lambda b: (b, qc + 1)),
                pl.BlockSpec((L_CTX, NA_DIM), lambda b: (b, qc + 2))]
    args = [h, h, h]
    aliases = {}
    if kv_prev is not None:
        in_specs.append(_any_spec())
        args.append(kv_prev)
        aliases = {3: 1}
    return pl.pallas_call(
        _ctx_attn_kernel,
        grid=(N_CTX_SEQ,),
        in_specs=in_specs,
        out_specs=[pl.BlockSpec((L_CTX, NA_DIM), lambda b: (b, 0)),
                   pl.BlockSpec((None, None, 2, NA_HEADS, L_CTX, NA_HEAD_DIM), lambda b: (b, j, 0, 0, 0, 0))],
        out_shape=[jax.ShapeDtypeStruct((T_ALL, NA_DIM), BF16),
                   jax.ShapeDtypeStruct((N_CTX_SEQ, N_EVEN, 2, NA_HEADS, L_CTX, NA_HEAD_DIM), F32)],
        input_output_aliases=aliases,
        compiler_params=_cparams(("parallel",)),
        name="context_attention",
    )(*args)


def _na_bias_table(rpb):
    col = np.arange(GRID_W)
    c0 = np.clip(col - WIN_C // 2, 0, GRID_W - WIN_C)
    col_ok = (col[None, :] >= c0[:, None]) & (col[None, :] < c0[:, None] + WIN_C)
    dc = np.clip(col[None, :] - col[:, None], -(WIN_C - 1), WIN_C - 1) + (WIN_C - 1)
    onehot = (dc.reshape(-1)[None, :] == np.arange(2 * WIN_C - 1)[:, None]).astype(np.float32)
    t = jnp.einsum("hrc,cn->hrn", rpb.astype(F32), onehot, precision=lax.Precision.HIGHEST)
    t = jnp.where(col_ok.reshape(-1), t, NEG_INF).reshape(NA_HEADS, 2 * WIN_R - 1, GRID_W, GRID_W)
    return jnp.stack([jnp.concatenate([t[:, d0 + w] for w in range(WIN_R)], axis=-1) for d0 in range(WIN_R)], axis=1)


def _na_attn_kernel(q_ref, k_ref, v_ref, kc_ref, vc_ref, bias_ref, o_in_ref, o_ref, kb, vb, kcb, vcb):
    del o_in_ref
    kb[...] = k_ref[...].astype(BF16)
    vb[...] = v_ref[...].astype(BF16)
    kcb[...] = kc_ref[...].astype(BF16)
    vcb[...] = vc_ref[...].astype(BF16)
    scale = NA_HEAD_DIM ** -0.5
    win = WIN_R * GRID_W

    def row(r, carry):
        r0 = jnp.clip(r - WIN_R // 2, 0, NA_ROWS - WIN_R)
        d0 = r0 - r + (WIN_R - 1)
        qs = pl.multiple_of(r * GRID_W, GRID_W)
        ks = pl.multiple_of(r0 * GRID_W, GRID_W)
        q = q_ref[pl.ds(qs, GRID_W), :].astype(BF16)
        s_loc = _dot_nt(q, kb[pl.ds(ks, win), :]) * scale + bias_ref[d0]
        s_ctx = _dot_nt(q, kcb[...]) * scale
        m = jnp.maximum(jnp.max(s_loc, axis=-1, keepdims=True), jnp.max(s_ctx, axis=-1, keepdims=True))
        e_loc = jnp.exp(s_loc - m)
        e_ctx = jnp.exp(s_ctx - m)
        den = jnp.sum(e_loc, axis=-1, keepdims=True) + jnp.sum(e_ctx, axis=-1, keepdims=True)
        o = _dot((e_loc / den).astype(BF16), vb[pl.ds(ks, win), :]) + _dot((e_ctx / den).astype(BF16), vcb[...])
        o_ref[pl.ds(qs, GRID_W), :] = o.astype(o_ref.dtype)
        return carry

    lax.fori_loop(0, NA_ROWS, row, 0, unroll=NA_UNROLL)


def _neighbourhood_attention(h, cache_kv, j, bias, o_prev):
    qc = 2 * SG_DIM // NA_HEAD_DIM
    rb = T_CTX // L_LAT
    cache_spec = lambda kv: pl.BlockSpec((None, None, None, None, PAST_LEN, NA_HEAD_DIM),
                                         lambda b, hh: (b, j, kv, hh, 0, 0))
    return pl.pallas_call(
        _na_attn_kernel,
        grid=(N_LAT_SEQ, NA_HEADS),
        in_specs=[pl.BlockSpec((L_LAT, NA_HEAD_DIM), lambda b, hh: (rb + b, qc + hh)),
                  pl.BlockSpec((L_LAT, NA_HEAD_DIM), lambda b, hh: (rb + b, qc + NA_HEADS + hh)),
                  pl.BlockSpec((L_LAT, NA_HEAD_DIM), lambda b, hh: (rb + b, qc + 2 * NA_HEADS + hh)),
                  cache_spec(0), cache_spec(1),
                  pl.BlockSpec((None, WIN_R, GRID_W, WIN_R * GRID_W), lambda b, hh: (hh, 0, 0, 0)),
                  _any_spec()],
        out_specs=pl.BlockSpec((L_LAT, NA_HEAD_DIM), lambda b, hh: (rb + b, hh)),
        out_shape=jax.ShapeDtypeStruct((T_ALL, NA_DIM), BF16),
        input_output_aliases={6: 0},
        scratch_shapes=[pltpu.VMEM((L_LAT, NA_HEAD_DIM), BF16), pltpu.VMEM((L_LAT, NA_HEAD_DIM), BF16),
                        pltpu.VMEM((PAST_LEN, NA_HEAD_DIM), BF16), pltpu.VMEM((PAST_LEN, NA_HEAD_DIM), BF16)],
        compiler_params=_cparams(("parallel", "parallel")),
        name="neighbourhood_attention",
    )(h, h, h, cache_kv, cache_kv, bias, o_prev)


CONV_ROWS = 256
CONV_COLS = 512


def _conv_kernel(transpose_out, x_ref, p_ref, n_ref, w_ref, b_ref, o_ref):
    i = pl.program_id(0)
    per = L_LAT // CONV_ROWS
    nct = T_CTX // CONV_ROWS
    pos = jnp.where(i < nct, 0, (i - nct) % per)
    last = jnp.where(i < nct, 0, per - 1)
    x = x_ref[...]
    prev_row = jnp.where(pos == 0, 0.0, p_ref[7:8, :])
    next_row = jnp.where(pos == last, 0.0, n_ref[0:1, :])
    ridx = lax.broadcasted_iota(jnp.int32, x.shape, 0)
    x_dn = jnp.where(ridx == 0, prev_row, pltpu.roll(x, 1, 0))
    x_up = jnp.where(ridx == CONV_ROWS - 1, next_row, pltpu.roll(x, CONV_ROWS - 1, 0))
    y = _silu(x_dn * w_ref[0:1, :] + x * w_ref[1:2, :] + x_up * w_ref[2:3, :] + b_ref[...])
    if transpose_out:
        for c in range(CONV_ROWS // SSD_CHUNK):
            o_ref[c] = y[c * SSD_CHUNK:(c + 1) * SSD_CHUNK, :].T
    else:
        o_ref[...] = y


def _conv_silu(h, conv_w, conv_b, col0, ncols, transpose_out):
    cb0 = (ODD_COL_XBC + col0) // CONV_COLS
    wb0 = col0 // CONV_COLS
    rb = CONV_ROWS // 8
    nblk8 = T_ALL // 8
    cpb = CONV_ROWS // SSD_CHUNK
    w8 = jnp.zeros((8, SSD_CONV_CH), F32).at[:3].set(conv_w)
    if transpose_out:
        out_spec = pl.BlockSpec((cpb, CONV_COLS, SSD_CHUNK), lambda i, j: (i, j, 0))
        out_shape = jax.ShapeDtypeStruct((T_ALL // SSD_CHUNK, ncols, SSD_CHUNK), F32)
    else:
        out_spec = pl.BlockSpec((CONV_ROWS, CONV_COLS), lambda i, j: (i, j))
        out_shape = jax.ShapeDtypeStruct((T_ALL, ncols), F32)
    return pl.pallas_call(
        functools.partial(_conv_kernel, transpose_out),
        grid=(T_ALL // CONV_ROWS, ncols // CONV_COLS),
        in_specs=[pl.BlockSpec((CONV_ROWS, CONV_COLS), lambda i, j: (i, cb0 + j)),
                  pl.BlockSpec((8, CONV_COLS), lambda i, j: (jnp.maximum(i * rb - 1, 0), cb0 + j)),
                  pl.BlockSpec((8, CONV_COLS), lambda i, j: (jnp.minimum((i + 1) * rb, nblk8 - 1), cb0 + j)),
                  pl.BlockSpec((8, CONV_COLS), lambda i, j: (0, wb0 + j)),
                  pl.BlockSpec((1, CONV_COLS), lambda i, j: (0, wb0 + j))],
        out_specs=out_spec,
        out_shape=out_shape,
        compiler_params=_cparams(("parallel", "parallel")),
        name="conv_silu_t" if transpose_out else "conv_silu",
    )(h, h, h, w8, conv_b.reshape(1, SSD_CONV_CH))


def _seq_info(t, chunk):
    nct = L_CTX // chunk
    ncl = L_LAT // chunk
    nctx = N_CTX_SEQ * nct
    is_ctx = t < nctx
    tl = jnp.maximum(t - nctx, 0)
    seq = jnp.where(is_ctx, t // nct, N_CTX_SEQ + tl // ncl)
    c = jnp.where(is_ctx, t % nct, tl % ncl)
    n = jnp.where(is_ctx, nct, ncl)
    first = jnp.where(is_ctx, (t // nct) * nct, nctx + (tl // ncl) * ncl)
    return seq, c, n, first


def _row_blk(t, chunk, rev):
    if not rev:
        return t
    _, c, n, first = _seq_info(t, chunk)
    return first + (n - 1 - c)


def _lat_seq(t, chunk):
    return jnp.maximum(_seq_info(t, chunk)[0] - N_CTX_SEQ, 0)


def _ctx_seq(t, chunk):
    return jnp.minimum(_seq_info(t, chunk)[0], N_CTX_SEQ - 1)


def _head_rows(mat, d, g):
    base = d * SSD_HEADS + g * SSD_GROUP_HEADS
    return jnp.concatenate(
        [jnp.broadcast_to(mat[base + r:base + r + 1, :], (SSD_HEAD_DIM, mat.shape[1])) for r in range(SSD_GROUP_HEADS)],
        axis=0)


def _ssd_kernel(rev, has_prev, *refs):
    refs = list(refs)
    xt_ref, bc_ref, sm_ref, dtb_ref, alog_ref, s0_ref = refs[:6]
    pos = 6
    if rev:
        yf_ref, z_ref, dcol_ref, gn_ref = refs[pos:pos + 4]
        pos += 4
    if has_prev:
        pos += 1
    y_ref, sfin_ref, s_scr = refs[pos:pos + 3]
    yt_scr = refs[pos + 3] if rev else None

    t = pl.program_id(0)
    seq, c, n, _ = _seq_info(t, SSD_CHUNK)
    is_ctx = seq < N_CTX_SEQ
    q = SSD_CHUNK
    d = 1 if rev else 0

    @pl.when(c == 0)
    def _():
        s_scr[...] = jnp.where(is_ctx, 0.0, s0_ref[...])

    ii = lax.broadcasted_iota(jnp.int32, (q, q), 0)
    jj = lax.broadcasted_iota(jnp.int32, (q, q), 1)
    mask = (jj >= ii) if rev else (jj <= ii)
    tri = mask.astype(BF16)

    dt = _softplus(sm_ref[...] + dtb_ref[...])
    dta = dt * (-jnp.exp(alog_ref[...]))
    cum = _mask_dot(tri, dta)
    tot = cum[0:1, :] if rev else cum[q - 1:q, :]
    cum_t = cum.T
    dt_t = dt.T
    te_t = (jnp.exp(tot - cum) * dt).T
    ecum_t = jnp.exp(cum_t)
    last = 0 if rev else q - 1
    etot_t = jnp.broadcast_to(ecum_t[:, last:last + 1], (q, q))

    for g in range(SSD_GROUPS):
        gs = slice(g * SSD_GROUP_DIM, (g + 1) * SSD_GROUP_DIM)
        bg = bc_ref[:, g * SSD_STATE:(g + 1) * SSD_STATE].astype(BF16)
        cg = bc_ref[:, SSD_BC + g * SSD_STATE:SSD_BC + (g + 1) * SSD_STATE].astype(BF16)
        cb = _dot_nt(cg, bg)
        s_g = s_scr[gs, :]
        x_g = xt_ref[gs, :]
        y_off = _dot_nt(s_g.astype(BF16), cg) * _head_rows(ecum_t, d, g)
        st = _dot((x_g * _head_rows(te_t, d, g)).astype(BF16), bg)
        s_scr[gs, :] = s_g * _head_rows(etot_t, d, g) + st
        for k in range(SSD_GROUP_HEADS // 2):
            w_pair = []
            for r in (2 * k, 2 * k + 1):
                col = d * SSD_HEADS + g * SSD_GROUP_HEADS + r
                seg = cum[:, col:col + 1] - cum_t[col:col + 1, :]
                w_pair.append((cb * jnp.where(mask, jnp.exp(seg), 0.0) * dt_t[col:col + 1, :]).astype(BF16))
            w2 = jnp.concatenate(w_pair, axis=1)
            ps = slice(k * 2 * SSD_HEAD_DIM, (k + 1) * 2 * SSD_HEAD_DIM)
            xp = x_g[ps, :].astype(BF16)
            rr = lax.broadcasted_iota(jnp.int32, xp.shape, 0)
            zero = jnp.zeros_like(xp)
            x2 = jnp.concatenate([jnp.where(rr < SSD_HEAD_DIM, xp, zero), jnp.where(rr >= SSD_HEAD_DIM, xp, zero)],
                                 axis=1)
            y_pair = _dot_nt(x2, w2) + y_off[ps, :]
            rows = slice(g * SSD_GROUP_DIM + k * 2 * SSD_HEAD_DIM, g * SSD_GROUP_DIM + (k + 1) * 2 * SSD_HEAD_DIM)
            if rev:
                yt_scr[rows, :] = y_pair
            else:
                y_ref[rows, :] = y_pair

    if rev:
        yt = yf_ref[...] + yt_scr[...] + dcol_ref[...] * xt_ref[...]
        y = yt.T * _silu(z_ref[...])
        y = y * lax.rsqrt(jnp.mean(y * y, axis=-1, keepdims=True) + EPS) * gn_ref[...]
        y_ref[...] = y.astype(y_ref.dtype)

    @pl.when((c == n - 1) & is_ctx)
    def _():
        sfin_ref[...] = s_scr[...]


def _ssd_scan(rev, layer, xt, bc, small, dtb, alog, s0, states_prev, extra=None):
    q = SSD_CHUNK
    d = 1 if rev else 0
    blk = lambda t: _row_blk(t, q, rev)
    in_specs = [pl.BlockSpec((None, SSD_DIM, q), lambda t: (blk(t), 0, 0)),
                pl.BlockSpec((q, 2 * SSD_BC), lambda t: (blk(t), 0)),
                pl.BlockSpec((q, SMALL_W), lambda t: (blk(t), 0)),
                pl.BlockSpec((1, SMALL_W), lambda t: (0, 0)),
                pl.BlockSpec((1, SMALL_W), lambda t: (0, 0)),
                pl.BlockSpec((None, SSD_DIM, SSD_STATE), lambda t: (_lat_seq(t, q), 0, 0))]
    args = [xt, bc, small, dtb, alog, s0]
    if rev:
        yf, h, dcol, gn = extra
        in_specs += [pl.BlockSpec((None, SSD_DIM, q), lambda t: (blk(t), 0, 0)),
                     pl.BlockSpec((q, SSD_DIM), lambda t: (blk(t), 0)),
                     pl.BlockSpec((SSD_DIM, q), lambda t: (0, 0)),
                     pl.BlockSpec((1, SSD_DIM), lambda t: (0, 0))]
        args += [yf, h, dcol, gn]
        y_spec = pl.BlockSpec((q, SSD_DIM), lambda t: (blk(t), 0))
        y_shape = jax.ShapeDtypeStruct((T_ALL, SSD_DIM), BF16)
    else:
        y_spec = pl.BlockSpec((None, SSD_DIM, q), lambda t: (blk(t), 0, 0))
        y_shape = jax.ShapeDtypeStruct((T_ALL // q, SSD_DIM, q), F32)
    aliases = {}
    if states_prev is not None:
        aliases = {len(args): 1}
        in_specs.append(_any_spec())
        args.append(states_prev)
    scratch = [pltpu.VMEM((SSD_DIM, SSD_STATE), F32)]
    if rev:
        scratch.append(pltpu.VMEM((SSD_DIM, q), F32))
    return pl.pallas_call(
        functools.partial(_ssd_kernel, rev, states_prev is not None),
        grid=(T_ALL // q,),
        in_specs=in_specs,
        out_specs=[y_spec,
                   pl.BlockSpec((None, None, None, SSD_DIM, SSD_STATE), lambda t: (_ctx_seq(t, q), layer, d, 0, 0))],
        out_shape=[y_shape, jax.ShapeDtypeStruct((N_CTX_SEQ, N_ODD, 2, SSD_DIM, SSD_STATE), F32)],
        input_output_aliases=aliases,
        scratch_shapes=scratch,
        compiler_params=_cparams(("arbitrary",)),
        name="ssd_bwd" if rev else "ssd_fwd",
    )(*args)


def _gla_kernel(rev, has_prev, *refs):
    refs = list(refs)
    q_ref, k_ref, v_ref, sm_ref, wa_ref, ba_ref, s0_ref = refs[:7]
    pos = 7
    if rev:
        of_ref, gate_ref, gn_ref = refs[pos:pos + 3]
        pos += 3
    if has_prev:
        pos += 1
    o_ref, sfin_ref, s_scr = refs[pos:pos + 3]

    t = pl.program_id(0)
    seq, c, n, _ = _seq_info(t, GLA_STEP)
    is_ctx = seq < N_CTX_SEQ
    cq = GLA_CHUNK
    sq = GLA_STEP

    @pl.when(c == 0)
    def _():
        for hh in range(GLA_HEADS):
            s_scr[hh] = jnp.where(is_ctx, 0.0, s0_ref[hh].T)

    ii = lax.broadcasted_iota(jnp.int32, (sq, sq), 0)
    jj = lax.broadcasted_iota(jnp.int32, (sq, sq), 1)
    same = (ii < cq) == (jj < cq)
    mask = same & ((jj >= ii) if rev else (jj <= ii))
    cross = ((ii < cq) & (jj >= cq)) if rev else ((ii >= cq) & (jj < cq))
    tri = mask.astype(BF16)
    row1 = lax.broadcasted_iota(jnp.int32, (sq, 1), 0)
    in_first = (row1 >= cq) if rev else (row1 < cq)

    g_lin = _dot(sm_ref[...].astype(BF16), wa_ref[...]) + ba_ref[...]
    g = -_softplus(-g_lin) / GLA_GATE_NORM
    cum = _mask_dot(tri, g)
    if rev:
        last_1, last_2 = cum[cq:cq + 1, :], cum[0:1, :]
    else:
        last_1, last_2 = cum[cq - 1:cq, :], cum[sq - 1:sq, :]
    e_last_1 = jnp.exp(last_1)
    e_last_2 = jnp.exp(last_2)
    q_e = (q_ref[...] * (GLA_HEAD_K ** -0.5)) * jnp.exp(cum)
    k = k_ref[...]
    k_e = k * jnp.exp(-cum)
    k_end = k * jnp.exp(jnp.where(in_first, last_1, last_2) - cum)
    q_s = jnp.where(in_first, q_e, q_e * e_last_1)
    k_w = jnp.where(in_first, k_end * e_last_2, k_end)
    e_step = e_last_1 * e_last_2

    for hh in range(GLA_HEADS):
        ks = slice(hh * GLA_HEAD_K, (hh + 1) * GLA_HEAD_K)
        vs = slice(hh * GLA_HEAD_V, (hh + 1) * GLA_HEAD_V)
        qh = q_e[:, ks].astype(BF16)
        vh = v_ref[:, vs]
        vh_b = vh.astype(BF16)
        s_prev = s_scr[hh]
        att = (jnp.where(mask, _dot_nt(qh, k_e[:, ks].astype(BF16)), 0.0)
               + jnp.where(cross, _dot_nt(qh, k_end[:, ks].astype(BF16)), 0.0))
        o = _dot(att.astype(BF16), vh_b) + _dot_nt(q_s[:, ks].astype(BF16), s_prev.astype(BF16))
        st = _dot(vh.T.astype(BF16), k_w[:, ks].astype(BF16))
        s_scr[hh] = s_prev * e_step[:, ks] + st
        if rev:
            o = o + of_ref[:, vs]
            o = o * lax.rsqrt(jnp.mean(o * o, axis=-1, keepdims=True) + EPS) * gn_ref[...]
            o_ref[:, vs] = (o * _silu(gate_ref[:, vs])).astype(o_ref.dtype)
        else:
            o_ref[:, vs] = o

    @pl.when((c == n - 1) & is_ctx)
    def _():
        for hh in range(GLA_HEADS):
            sfin_ref[hh] = s_scr[hh].T


def _gla_scan(rev, layer, h, small, wa, ba, s0, states_prev, extra=None):
    cq = GLA_STEP
    d = 1 if rev else 0
    cv = SSD_DIM // GLA_V_DIM
    cqk = (SSD_DIM + 2 * GLA_V_DIM) // GLA_QK_DIM
    row = lambda t: _row_blk(t, cq, rev)
    in_specs = [pl.BlockSpec((cq, GLA_QK_DIM), lambda t: (row(t), cqk)),
                pl.BlockSpec((cq, GLA_QK_DIM), lambda t: (row(t), cqk + 1)),
                pl.BlockSpec((cq, GLA_V_DIM), lambda t: (row(t), cv)),
                pl.BlockSpec((cq, SMALL_W), lambda t: (row(t), 0)),
                pl.BlockSpec((SMALL_W, GLA_QK_DIM), lambda t: (0, 0)),
                pl.BlockSpec((1, GLA_QK_DIM), lambda t: (0, 0)),
                pl.BlockSpec((None, GLA_HEADS, GLA_HEAD_K, GLA_HEAD_V), lambda t: (_lat_seq(t, cq), 0, 0, 0))]
    args = [h, h, h, small, wa, ba, s0]
    if rev:
        o_f, gn = extra
        in_specs += [pl.BlockSpec((cq, GLA_V_DIM), lambda t: (row(t), 0)),
                     pl.BlockSpec((cq, GLA_V_DIM), lambda t: (row(t), cv + 1)),
                     pl.BlockSpec((1, GLA_HEAD_V), lambda t: (0, 0))]
        args += [o_f, h, gn]
    aliases = {}
    if states_prev is not None:
        aliases = {len(args): 1}
        in_specs.append(_any_spec())
        args.append(states_prev)
    return pl.pallas_call(
        functools.partial(_gla_kernel, rev, states_prev is not None),
        grid=(T_ALL // cq,),
        in_specs=in_specs,
        out_specs=[pl.BlockSpec((cq, GLA_V_DIM), lambda t: (row(t), 0)),
                   pl.BlockSpec((None, None, None, GLA_HEADS, GLA_HEAD_K, GLA_HEAD_V),
                                lambda t: (_ctx_seq(t, cq), layer, d, 0, 0, 0))],
        out_shape=[jax.ShapeDtypeStruct((T_ALL, GLA_V_DIM), BF16 if rev else F32),
                   jax.ShapeDtypeStruct((N_CTX_SEQ, N_ODD, 2, GLA_HEADS, GLA_HEAD_K, GLA_HEAD_V), F32)],
        input_output_aliases=aliases,
        scratch_shapes=[pltpu.VMEM((GLA_HEADS, GLA_HEAD_V, GLA_HEAD_K), F32)],
        compiler_params=_cparams(("arbitrary",)),
        name="gla_bwd" if rev else "gla_fwd",
    )(*args)


ROUTER_ROWS = 32


def _first_max(vals):
    out = []
    for a, va in enumerate(vals):
        ok = None
        for b, vb in enumerate(vals):
            if a == b:
                continue
            t = (va > vb) if b < a else (va >= vb)
            ok = t if ok is None else (ok & t)
        out.append(ok)
    return out


def _router_kernel(x_ref, g_ref, sc_ref, sh_ref, whi_ref, wlo_ref, bsel_ref, tri_ref,
                   hn_ref, gates_ref, rank_ref, cnt_ref, carry):
    i = pl.program_id(0)

    @pl.when(i == 0)
    def _():
        carry[...] = jnp.zeros_like(carry)

    hn = _norm_mod(x_ref[...], g_ref[...], sc_ref[...], sh_ref[...])
    hn_ref[...] = hn
    hb = hn.astype(BF16)
    hl = (hn - hb.astype(F32)).astype(BF16)
    logits = _dot(hb, whi_ref[...]) + _dot(hb, wlo_ref[...]) + _dot(hl, whi_ref[...])
    lt = logits.T
    score = [jax.nn.sigmoid(lt[8 * m:8 * m + 8, :]) for m in range(EXPERTS_PER_GROUP)]
    sel = [score[m] + bsel_ref[8 * m:8 * m + 8, :] for m in range(EXPERTS_PER_GROUP)]

    hi1, lo1 = jnp.maximum(sel[0], sel[1]), jnp.minimum(sel[0], sel[1])
    hi2, lo2 = jnp.maximum(sel[2], sel[3]), jnp.minimum(sel[2], sel[3])
    grp = jnp.maximum(hi1, hi2) + jnp.maximum(jnp.minimum(hi1, hi2), jnp.maximum(lo1, lo2))
    best = _first_max([grp[g:g + 1, :] for g in range(N_EXPERT_GROUPS)])
    srow = lax.broadcasted_iota(jnp.int32, grp.shape, 0)
    gmask = jnp.zeros(grp.shape, jnp.bool_)
    for g in range(N_EXPERT_GROUPS):
        gmask = gmask | ((srow == g) & best[g])

    first = _first_max(sel)
    sel2 = [jnp.where(first[m], -jnp.inf, sel[m]) for m in range(EXPERTS_PER_GROUP)]
    second = _first_max(sel2)
    chosen = [(first[m] | second[m]) & gmask for m in range(EXPERTS_PER_GROUP)]
    wsum = sum(jnp.where(chosen[m], score[m], 0.0) for m in range(EXPERTS_PER_GROUP))
    wtot = jnp.sum(wsum, axis=0, keepdims=True)
    cm = jnp.concatenate([chosen[m].astype(F32) for m in range(EXPERTS_PER_GROUP)], axis=0)
    sc_all = jnp.concatenate(score, axis=0)
    gates_ref[...] = jnp.where(cm > 0, sc_all / wtot, 0.0)
    within = _dot(cm.astype(BF16), tri_ref[...])
    rank_ref[...] = jnp.where(cm > 0, carry[:, 0:1] + within, -1.0)
    carry[...] = carry[...] + jnp.sum(cm, axis=1, keepdims=True)
    cnt_ref[...] = carry[...]


def _router(x, g, sc, sh, w_router, b_router):
    slot_of_expert = np.array([(e % EXPERTS_PER_GROUP) * 8 + e // EXPERTS_PER_GROUP for e in range(N_EXPERTS)])
    w128 = jnp.zeros((D_MODEL, 128), F32).at[:, slot_of_expert].set(w_router)
    w_hi = w128.astype(BF16)
    w_lo = (w128 - w_hi.astype(F32)).astype(BF16)
    b32 = jnp.zeros((ROUTER_ROWS,), F32).at[slot_of_expert].set(b_router.astype(F32))
    bsel = jnp.broadcast_to(b32[:, None], (ROUTER_ROWS, TM))
    tri = jnp.asarray(np.triu(np.ones((TM, TM), np.float32), 1), BF16)
    mod_spec = pl.BlockSpec((None, 1, D_MODEL), lambda i: (_mod_row(i, TM), 0, 0))
    hn, gates, rank, cnt = pl.pallas_call(
        _router_kernel,
        grid=(T_ALL // TM,),
        in_specs=[pl.BlockSpec((TM, D_MODEL), lambda i: (i, 0)),
                  pl.BlockSpec((1, D_MODEL), lambda i: (0, 0)),
                  mod_spec, mod_spec,
                  pl.BlockSpec((D_MODEL, 128), lambda i: (0, 0)),
                  pl.BlockSpec((D_MODEL, 128), lambda i: (0, 0)),
                  pl.BlockSpec((ROUTER_ROWS, TM), lambda i: (0, 0)),
                  pl.BlockSpec((TM, TM), lambda i: (0, 0))],
        out_specs=[pl.BlockSpec((TM, D_MODEL), lambda i: (i, 0)),
                   pl.BlockSpec((ROUTER_ROWS, TM), lambda i: (0, i)),
                   pl.BlockSpec((ROUTER_ROWS, TM), lambda i: (0, i)),
                   pl.BlockSpec((ROUTER_ROWS, 128), lambda i: (0, 0))],
        out_shape=[jax.ShapeDtypeStruct((T_ALL, D_MODEL), F32),
                   jax.ShapeDtypeStruct((ROUTER_ROWS, T_ALL), F32),
                   jax.ShapeDtypeStruct((ROUTER_ROWS, T_ALL), F32),
                   jax.ShapeDtypeStruct((ROUTER_ROWS, 128), F32)],
        scratch_shapes=[pltpu.VMEM((ROUTER_ROWS, 128), F32)],
        compiler_params=_cparams(("arbitrary",)),
        name="router",
    )(x, g.reshape(1, D_MODEL), sc, sh, w_hi, w_lo, bsel, tri)
    return hn, gates[slot_of_expert], rank[slot_of_expert], cnt[slot_of_expert, 0]


def _dispatch_plan(gates, rank, cnt):
    cnt = cnt.astype(jnp.int32)
    padded = ((cnt + MOE_TM - 1) // MOE_TM) * MOE_TM
    end = jnp.cumsum(padded)
    start = end - padded
    chosen = rank >= 0
    slot = start[:, None] + rank.astype(jnp.int32)
    slot_a = jnp.min(jnp.where(chosen, slot, MOE_ROWS), axis=0)
    slot_b = jnp.max(jnp.where(chosen, slot, -1), axis=0)
    w_a = jnp.sum(jnp.where(chosen & (slot == slot_a[None]), gates, 0.0), axis=0)
    w_b = jnp.sum(jnp.where(chosen & (slot == slot_b[None]), gates, 0.0), axis=0)
    tile_row = jnp.arange(MOE_TILES, dtype=jnp.int32) * MOE_TM
    tile_expert = jnp.minimum(jnp.sum(tile_row[:, None] >= end[None, :], axis=1), N_EXPERTS - 1).astype(jnp.int32)
    changed = jnp.concatenate([jnp.ones((1,), jnp.bool_), tile_expert[1:] != tile_expert[:-1]])
    tile_state = jnp.where(tile_row < end[-1], 1 + changed.astype(jnp.int32), 0).astype(jnp.int32)
    tidx = jnp.arange(MOE_TILES, dtype=jnp.int32)
    later_first = (tidx[None, :] > tidx[:, None]) & (tile_state[None, :] == 2)
    nxt = jnp.min(jnp.where(later_first, tidx[None, :], MOE_TILES), axis=1)
    next_expert = jnp.where(nxt < MOE_TILES, tile_expert[jnp.minimum(nxt, MOE_TILES - 1)], -1).astype(jnp.int32)
    return slot_a, slot_b, w_a, w_b, tile_expert, tile_state, next_expert


def _dispatch_kernel(sa_ref, sb_ref, hn_ref, xs_in_ref, xs_ref, sem):
    del xs_in_ref
    base = pl.program_id(0) * TM

    def row_copies(r):
        src = hn_ref.at[pl.ds(r, 1)]
        return (pltpu.make_async_copy(src, xs_ref.at[pl.ds(sa_ref[base + r], 1)], sem),
                pltpu.make_async_copy(src, xs_ref.at[pl.ds(sb_ref[base + r], 1)], sem))

    def start(r, carry):
        for cp in row_copies(r):
            cp.start()
        return carry

    def wait(r, carry):
        for cp in row_copies(r):
            cp.wait()
        return carry

    lax.fori_loop(0, TM, start, 0, unroll=DISPATCH_UNROLL)
    lax.fori_loop(0, TM, wait, 0, unroll=DISPATCH_UNROLL)


def _dispatch(hn, slot_a, slot_b, xs0):
    return pl.pallas_call(
        _dispatch_kernel,
        grid_spec=pltpu.PrefetchScalarGridSpec(
            num_scalar_prefetch=2,
            grid=(T_ALL // TM,),
            in_specs=[pl.BlockSpec((TM, D_MODEL), lambda i, sa, sb: (i, 0)), _any_spec()],
            out_specs=_any_spec(),
            scratch_shapes=[pltpu.SemaphoreType.DMA(())]),
        out_shape=jax.ShapeDtypeStruct((MOE_ROWS, D_MODEL), F32),
        input_output_aliases={3: 0},
        compiler_params=_cparams(("arbitrary",)),
        name="moe_dispatch",
    )(slot_a, slot_b, hn, xs0)


def _moe_kernel(layer, te_ref, ts_ref, nx_ref, x_ref, wg_hbm, wu_hbm, wd_hbm, o_ref,
                wg_f, wu_f, wd_f, wg_b, wu_b, wd_b, sem):
    i = pl.program_id(0)
    state = ts_ref[i]

    def weight_copies(e):
        return (pltpu.make_async_copy(wg_hbm.at[layer, e], wg_f, sem.at[0]),
                pltpu.make_async_copy(wu_hbm.at[layer, e], wu_f, sem.at[1]),
                pltpu.make_async_copy(wd_hbm.at[layer, e], wd_f, sem.at[2]))

    @pl.when(i == 0)
    def _():
        for cp in weight_copies(te_ref[0]):
            cp.start()

    @pl.when(state == 2)
    def _():
        cg, cu, cd = weight_copies(te_ref[i])
        cg.wait()
        wg_b[...] = wg_f[...].astype(BF16)
        cu.wait()
        wu_b[...] = wu_f[...].astype(BF16)
        cd.wait()
        wd_b[...] = wd_f[...].astype(BF16)

        @pl.when(nx_ref[i] >= 0)
        def _():
            for cp in weight_copies(nx_ref[i]):
                cp.start()

    @pl.when(state > 0)
    def _():
        x = x_ref[...].astype(BF16)
        act = (_silu(_dot(x, wg_b[...])) * _dot(x, wu_b[...])).astype(BF16)
        o_ref[...] = _dot(act, wd_b[...])

    @pl.when(state == 0)
    def _():
        o_ref[...] = jnp.zeros_like(o_ref)


def _moe_experts(layer, tile_expert, tile_state, next_expert, xs, w_gate, w_up, w_down):
    return pl.pallas_call(
        functools.partial(_moe_kernel, layer),
        grid_spec=pltpu.PrefetchScalarGridSpec(
            num_scalar_prefetch=3,
            grid=(MOE_TILES,),
            in_specs=[pl.BlockSpec((MOE_TM, D_MODEL), lambda i, te, ts, nx: (i, 0)),
                      _any_spec(), _any_spec(), _any_spec()],
            out_specs=pl.BlockSpec((MOE_TM, D_MODEL), lambda i, te, ts, nx: (i, 0)),
            scratch_shapes=[pltpu.VMEM((D_MODEL, D_FF), F32), pltpu.VMEM((D_MODEL, D_FF), F32),
                            pltpu.VMEM((D_FF, D_MODEL), F32),
                            pltpu.VMEM((D_MODEL, D_FF), BF16), pltpu.VMEM((D_MODEL, D_FF), BF16),
                            pltpu.VMEM((D_FF, D_MODEL), BF16),
                            pltpu.SemaphoreType.DMA((3,))]),
        out_shape=jax.ShapeDtypeStruct((MOE_ROWS, D_MODEL), F32),
        compiler_params=_cparams(("arbitrary",)),
        name="moe_experts",
    )(tile_expert, tile_state, next_expert, xs, w_gate, w_up, w_down)


def _combine_kernel(x_ref, g_ref, ya_ref, yb_ref, wa_ref, wb_ref, o_ref):
    o_ref[...] = x_ref[...] + g_ref[...] * (wa_ref[...] * ya_ref[...] + wb_ref[...] * yb_ref[...])


def _combine(x, gate, ya, yb, w_a, w_b):
    row = pl.BlockSpec((TM, D_MODEL), lambda i: (i, 0))
    wspec = pl.BlockSpec((TM, 1), lambda i: (i, 0))
    return pl.pallas_call(
        _combine_kernel,
        grid=(T_ALL // TM,),
        in_specs=[row, pl.BlockSpec((None, 1, D_MODEL), lambda i: (_mod_row(i, TM), 0, 0)), row, row, wspec, wspec],
        out_specs=row,
        out_shape=jax.ShapeDtypeStruct((T_ALL, D_MODEL), F32),
        compiler_params=_cparams(("parallel",)),
        name="moe_combine",
    )(x, gate, ya, yb, w_a.reshape(T_ALL, 1), w_b.reshape(T_ALL, 1))


def _take_rows(a, idx):
    return a.at[idx].get(mode="promise_in_bounds")


def _moe_layer(layer, x, xs_buf, g, sc, sh, gate, w_router, b_router, w_gate, w_up, w_down):
    hn, gates, rank, cnt = _router(x, g, sc, sh, w_router, b_router)
    slot_a, slot_b, w_a, w_b, tile_expert, tile_state, next_expert = _dispatch_plan(gates, rank, cnt)
    xs = _dispatch(hn, slot_a, slot_b, xs_buf)
    y = _moe_experts(layer, tile_expert, tile_state, next_expert, xs, w_gate, w_up, w_down)
    return _combine(x, gate, _take_rows(y, slot_a), _take_rows(y, slot_b), w_a, w_b), xs


def _final_norm_kernel(x_ref, g_ref, oc_ref, ol_ref):
    x = x_ref[...]
    y = x * lax.rsqrt(jnp.mean(x * x, axis=-1, keepdims=True) + EPS) * g_ref[...]
    is_ctx = pl.program_id(0) < T_CTX // TM

    @pl.when(is_ctx)
    def _():
        oc_ref[...] = y

    @pl.when(jnp.logical_not(is_ctx))
    def _():
        ol_ref[...] = y


def _final_norm(x, g):
    nct = T_CTX // TM
    return pl.pallas_call(
        _final_norm_kernel,
        grid=(T_ALL // TM,),
        in_specs=[pl.BlockSpec((TM, D_MODEL), lambda i: (i, 0)), pl.BlockSpec((1, D_MODEL), lambda i: (0, 0))],
        out_specs=[pl.BlockSpec((TM, D_MODEL), lambda i: (jnp.minimum(i, nct - 1), 0)),
                   pl.BlockSpec((TM, D_MODEL), lambda i: (jnp.maximum(i - nct, 0), 0))],
        out_shape=[jax.ShapeDtypeStruct((T_CTX, D_MODEL), F32), jax.ShapeDtypeStruct((T_LAT, D_MODEL), F32)],
        compiler_params=_cparams(("arbitrary",)),
        name="final_norm",
    )(x, g.reshape(1, D_MODEL))


def _even_layer(x, mods, j, cache_kv, kv_new, norm_g, w_in, w_out, sg_norm, sg_w, sg_b, rpb):
    sh1, sc1, g1 = mods
    h = _inproj(x, norm_g, sc1, sh1, w_in, "inproj_even")
    a_out = _spatial_gating(h, sg_norm, sg_w, sg_b)
    o, kv_new = _context_attention(h, j, kv_new)
    o = _neighbourhood_attention(h, cache_kv, j, _na_bias_table(rpb), o)
    return _outproj(a_out, o, w_out, x, g1, "outproj_even"), kv_new


def _odd_layer(x, mods, j, state_ssm, state_gla, ssm_new, gla_new, norm_g, w_in, w_out, conv_w, conv_b,
               dt_bias, a_log, ssd_d, ssd_norm, w_a2, b_a, gla_norm):
    sh1, sc1, g1 = mods
    w_main, w_small = _relayout_odd_weights(w_in, j)
    h = _inproj(x, norm_g, sc1, sh1, w_main, "inproj_odd", w_is_nk=True)
    small = _inproj(x, norm_g, sc1, sh1, w_small, "inproj_odd_small", w_is_nk=True)

    xt = _conv_silu(h, conv_w, conv_b, 0, SSD_DIM, True)
    bc = _conv_silu(h, conv_w, conv_b, SSD_DIM, 2 * SSD_BC, False)
    pad = jnp.zeros((SMALL_W - 2 * SSD_HEADS,), F32)
    dtb = jnp.concatenate([dt_bias.reshape(-1).astype(F32), pad]).reshape(1, SMALL_W)
    alog = jnp.concatenate([a_log.reshape(-1).astype(F32), pad]).reshape(1, SMALL_W)
    s0 = state_ssm.reshape(N_LAT_SEQ, 2, SSD_DIM, SSD_STATE)
    y_f, ssm_new = _ssd_scan(False, j, xt, bc, small, dtb, alog, s0[:, 0], ssm_new)
    dcol = jnp.broadcast_to(jnp.repeat(ssd_d.astype(F32), SSD_HEAD_DIM)[:, None], (SSD_DIM, SSD_CHUNK))
    y_c, ssm_new = _ssd_scan(True, j, xt, bc, small, dtb, alog, s0[:, 1], ssm_new,
                             (y_f, h, dcol, ssd_norm.reshape(1, SSD_DIM)))

    def wa_pad(dd):
        lo = 2 * SSD_HEADS + dd * GLA_RANK
        return jnp.zeros((SMALL_W, GLA_QK_DIM), F32).at[lo:lo + GLA_RANK].set(w_a2[dd]).astype(BF16)

    o_f, gla_new = _gla_scan(False, j, h, small, wa_pad(0), b_a[0].reshape(1, GLA_QK_DIM), state_gla[:, 0], gla_new)
    y_d, gla_new = _gla_scan(True, j, h, small, wa_pad(1), b_a[1].reshape(1, GLA_QK_DIM), state_gla[:, 1], gla_new,
                             (o_f, gla_norm.reshape(1, GLA_HEAD_V)))

    return _outproj(y_c, y_d, w_out, x, g1, "outproj_odd"), ssm_new, gla_new


def kernel(x_prompt, x_sample, cache_kv, state_ssm, state_gla, c, c_ctx, w_mod, b_mod, norm_mix, norm_ffn,
           norm_final, w_in_even, w_out_even, sg_norm, sg_w, sg_b, na_rpb, w_in_odd, w_out_odd, ssd_conv_w,
           ssd_conv_b, ssd_dt_bias, ssd_a_log, ssd_d, ssd_norm, gla_w_a2, gla_b_a, gla_norm, w_router, b_router,
           w_gate, w_up, w_down):
    x = jnp.concatenate([x_prompt.reshape(T_CTX, D_MODEL), x_sample.reshape(T_LAT, D_MODEL)], axis=0)
    c_rows = jnp.concatenate([c_ctx[None, :], c, jnp.zeros((N_MOD_ROWS - 1 - N_LAT_SEQ, D_MODEL), F32)], axis=0)
    mod = _modulation(c_rows, w_mod, b_mod)

    kv_new, ssm_new, gla_new = None, None, None
    xs_buf = jnp.zeros((MOE_ROWS, D_MODEL), F32)
    w_in_odd_nk = jnp.transpose(w_in_odd, (0, 2, 1))
    for i in range(DEPTH):
        m = mod[i, :1 + N_LAT_SEQ].reshape(1 + N_LAT_SEQ, N_MOD, 1, D_MODEL)
        sh1, sc1, g1, sh2, sc2, g2 = [m[:, s] for s in range(N_MOD)]
        j = i // 2
        if i % 2 == 0:
            x, kv_new = _even_layer(x, (sh1, sc1, g1), j, cache_kv, kv_new, norm_mix[i], w_in_even[j],
                                    w_out_even[j], sg_norm[j], sg_w[j], sg_b[j], na_rpb[j])
        else:
            x, ssm_new, gla_new = _odd_layer(x, (sh1, sc1, g1), j, state_ssm[:, j], state_gla[:, j], ssm_new,
                                             gla_new, norm_mix[i], w_in_odd_nk, w_out_odd[j], ssd_conv_w[j],
                                             ssd_conv_b[j], ssd_dt_bias[j], ssd_a_log[j], ssd_d[j], ssd_norm[j],
                                             gla_w_a2[j], gla_b_a[j], gla_norm[j])
        x, xs_buf = _moe_layer(i, x, xs_buf, norm_ffn[i], sc2, sh2, g2, w_router, b_router, w_gate, w_up, w_down)

    y_ctx, y_lat = _final_norm(x, norm_final)
    return (y_ctx.reshape(N_CTX_SEQ, L_CTX, D_MODEL),
            y_lat.reshape(N_LAT_SEQ, L_LAT, D_MODEL),
            kv_new,
            ssm_new.reshape(N_CTX_SEQ, N_ODD, 2, SSD_HEADS, SSD_HEAD_DIM, SSD_STATE),
            gla_new)
```
